```python
import jax, jax.numpy as jnp
from jax import lax
import numpy as np

D_MODEL = 4096
BATCH = 2
SEQ = 8192
DEPTH = 4

GRID_W = 64
N_MIXERS = 2
Q_BLOCK = 128
ROPE_THETA = 10000.0
EPS = 1e-6
D_FF = 3 * D_MODEL // 2
HEAD_DIM = 128
TOK_W = 3 * D_MODEL // 4
A_HEADS = TOK_W // HEAD_DIM
A_NOPE = HEAD_DIM
A_ROPE = 64
A_V = HEAD_DIM
A_Q_LORA = D_MODEL // 4
A_KV_LORA = D_MODEL // 8
B_HEADS = TOK_W // HEAD_DIM
B_KV_HEADS = B_HEADS // 4
B_HD = HEAD_DIM
MEM_TOKENS = 256
MEM_HEADS = 4
MEM_HD = D_MODEL // (4 * MEM_HEADS)
MEM_W = MEM_HEADS * MEM_HD
MIX_W = TOK_W + MEM_W
A_IN = A_Q_LORA + A_KV_LORA + A_ROPE + MEM_W
B_IN = B_HEADS * B_HD + 2 * B_KV_HEADS * B_HD + MEM_W
N_A = (DEPTH + N_MIXERS - 1) // N_MIXERS
N_B = DEPTH // N_MIXERS

kernel_name = "hybrid_mla_axial_gqa_macaron_memory"


def rms_norm(x, g):
    xf = x.astype(jnp.float32)
    y = xf * lax.rsqrt(jnp.mean(xf * xf, axis=-1, keepdims=True) + EPS)
    return (y * g.astype(jnp.float32)).astype(x.dtype)


def swiglu(x, w_gu, w_down):
    g, up = jnp.split(x @ w_gu, 2, axis=-1)
    return (jax.nn.silu(g) * up) @ w_down


def axial_rope_tables(seq_len, rot_dim):
    rows = seq_len // GRID_W
    row, col = jnp.meshgrid(jnp.arange(rows), jnp.arange(GRID_W), indexing="ij")
    row = row.reshape(-1).astype(jnp.float32)
    col = col.reshape(-1).astype(jnp.float32)
    axis_dim = rot_dim // 2
    inv_freq = ROPE_THETA ** (-jnp.arange(0, axis_dim, 2, dtype=jnp.float32) / axis_dim)
    ang = jnp.concatenate([row[:, None] * inv_freq, col[:, None] * inv_freq], axis=-1)
    return jnp.cos(ang), jnp.sin(ang)


def apply_rope(x, cos, sin):
    xf = x.astype(jnp.float32).reshape(x.shape[:-1] + (x.shape[-1] // 2, 2))
    x0, x1 = xf[..., 0], xf[..., 1]
    out = jnp.stack([x0 * cos - x1 * sin, x0 * sin + x1 * cos], axis=-1)
    return out.reshape(x.shape).astype(x.dtype)


def mla_attention(q_nope, q_pe, k_nope, k_pe, v):
    b, s, h, _ = q_nope.shape
    nb = s // Q_BLOCK
    qn = q_nope.reshape(b, nb, Q_BLOCK, h, A_NOPE).transpose(1, 0, 2, 3, 4)
    qp = q_pe.reshape(b, nb, Q_BLOCK, h, A_ROPE).transpose(1, 0, 2, 3, 4)
    scale = (A_NOPE + A_ROPE) ** -0.5

    def block(args):
        qn_b, qp_b = args
        sc = jnp.einsum("bqhd,bkhd->bhqk", qn_b, k_nope) + jnp.einsum("bqhr,bkr->bhqk", qp_b, k_pe)
        p = jax.nn.softmax(sc.astype(jnp.float32) * scale, axis=-1).astype(v.dtype)
        return jnp.einsum("bhqk,bkhd->bqhd", p, v)

    o = lax.map(block, (qn, qp))
    return o.transpose(1, 0, 2, 3, 4).reshape(b, s, h * A_V)


def gqa_attention(q, k, v):
    b, s, h, d = q.shape
    kvh = k.shape[2]
    g = h // kvh
    nb = s // Q_BLOCK
    qb = q.reshape(b, nb, Q_BLOCK, kvh, g, d).transpose(1, 0, 2, 3, 4, 5)
    scale = d ** -0.5

    def block(q_b):
        sc = jnp.einsum("bqkgd,bskd->bkgqs", q_b, k)
        p = jax.nn.softmax(sc.astype(jnp.float32) * scale, axis=-1).astype(v.dtype)
        return jnp.einsum("bkgqs,bskd->bqkgd", p, v)

    o = lax.map(block, qb)
    return o.transpose(1, 0, 2, 3, 4, 5).reshape(b, s, h * d)


def memory_attention(q, k, v):
    b, s = q.shape[:2]
    sc = jnp.einsum("bshd,bmhd->bhsm", q, k)
    p = jax.nn.softmax(sc.astype(jnp.float32) * (MEM_HD ** -0.5), axis=-1).astype(v.dtype)
    return jnp.einsum("bhsm,bmhd->bshd", p, v).reshape(b, s, MEM_W)


def mla_mixer(u, w_in, q_a_norm, kv_a_norm, w_q_b, w_kv_b,
              q_nope_norm, q_pe_norm, k_nope_norm, k_pe_norm, cos, sin):
    b, s, _ = u.shape
    proj = u @ w_in
    c_q, c_kv, k_pe, q_mem = jnp.split(
        proj, [A_Q_LORA, A_Q_LORA + A_KV_LORA, A_Q_LORA + A_KV_LORA + A_ROPE], axis=-1)
    q = (rms_norm(c_q, q_a_norm) @ w_q_b).reshape(b, s, A_HEADS, A_NOPE + A_ROPE)
    q_nope = rms_norm(q[..., :A_NOPE], q_nope_norm)
    q_pe = apply_rope(rms_norm(q[..., A_NOPE:], q_pe_norm), cos[:, None, :], sin[:, None, :])
    kv = (rms_norm(c_kv, kv_a_norm) @ w_kv_b).reshape(b, s, A_HEADS, A_NOPE + A_V)
    k_nope = rms_norm(kv[..., :A_NOPE], k_nope_norm)
    v = kv[..., A_NOPE:]
    k_pe = apply_rope(rms_norm(k_pe, k_pe_norm), cos, sin)
    return mla_attention(q_nope, q_pe, k_nope, k_pe, v), q_mem


def gqa_mixer(u, w_in, q_norm, k_norm, cos, sin):
    b, s, _ = u.shape
    proj = u @ w_in
    qw, kw = B_HEADS * B_HD, B_KV_HEADS * B_HD
    q, k, v, q_mem = jnp.split(proj, [qw, qw + kw, qw + 2 * kw], axis=-1)
    q = apply_rope(rms_norm(q.reshape(b, s, B_HEADS, B_HD), q_norm), cos[:, None, :], sin[:, None, :])
    k = apply_rope(rms_norm(k.reshape(b, s, B_KV_HEADS, B_HD), k_norm), cos[:, None, :], sin[:, None, :])
    v = v.reshape(b, s, B_KV_HEADS, B_HD)
    return gqa_attention(q, k, v), q_mem


def setup_inputs(seed: int = 0) -> dict:
    key = jax.random.key(seed)
    ks = jax.random.split(key, 26)
    f32 = jnp.float32

    def w(k, shape, fan_in):
        return jax.random.normal(k, shape, f32) * (fan_in ** -0.5)

    def g(k, shape):
        return 1.0 + 0.01 * jax.random.normal(k, shape, f32)

    return {
        "x": jax.random.normal(ks[0], (BATCH, SEQ, D_MODEL), f32),
        "mem": jax.random.normal(ks[1], (BATCH, MEM_TOKENS, D_MODEL), f32),
        "ffn1_norm": g(ks[2], (DEPTH, D_MODEL)),
        "ffn1_w_gu": w(ks[3], (DEPTH, D_MODEL, 2 * D_FF), D_MODEL),
        "ffn1_w_down": w(ks[4], (DEPTH, D_FF, D_MODEL), D_FF),
        "mix_norm": g(ks[5], (DEPTH, D_MODEL)),
        "w_o": w(ks[6], (DEPTH, MIX_W, D_MODEL), MIX_W),
        "mem_norm": g(ks[7], (DEPTH, D_MODEL)),
        "w_mem_kv": w(ks[8], (DEPTH, D_MODEL, 2 * MEM_W), D_MODEL),
        "mem_q_norm": g(ks[9], (DEPTH, MEM_HD)),
        "mem_k_norm": g(ks[10], (DEPTH, MEM_HD)),
        "ffn2_norm": g(ks[11], (DEPTH, D_MODEL)),
        "ffn2_w_gu": w(ks[12], (DEPTH, D_MODEL, 2 * D_FF), D_MODEL),
        "ffn2_w_down": w(ks[13], (DEPTH, D_FF, D_MODEL), D_FF),
        "a_w_in": w(ks[14], (N_A, D_MODEL, A_IN), D_MODEL),
        "a_q_a_norm": g(ks[15], (N_A, A_Q_LORA)),
        "a_kv_a_norm": g(ks[16], (N_A, A_KV_LORA)),
        "a_w_q_b": w(ks[17], (N_A, A_Q_LORA, A_HEADS * (A_NOPE + A_ROPE)), A_Q_LORA),
        "a_w_kv_b": w(ks[18], (N_A, A_KV_LORA, A_HEADS * (A_NOPE + A_V)), A_KV_LORA),
        "a_q_nope_norm": g(ks[19], (N_A, A_NOPE)),
        "a_q_pe_norm": g(ks[20], (N_A, A_ROPE)),
        "a_k_nope_norm": g(ks[21], (N_A, A_NOPE)),
        "a_k_pe_norm": g(ks[22], (N_A, A_ROPE)),
        "b_w_in": w(ks[23], (N_B, D_MODEL, B_IN), D_MODEL),
        "b_q_norm": g(ks[24], (N_B, B_HD)),
        "b_k_norm": g(ks[25], (N_B, B_HD)),
    }


def reference(x, mem, ffn1_norm, ffn1_w_gu, ffn1_w_down, mix_norm, w_o,
              mem_norm, w_mem_kv, mem_q_norm, mem_k_norm,
              ffn2_norm, ffn2_w_gu, ffn2_w_down,
              a_w_in, a_q_a_norm, a_kv_a_norm, a_w_q_b, a_w_kv_b,
              a_q_nope_norm, a_q_pe_norm, a_k_nope_norm, a_k_pe_norm,
              b_w_in, b_q_norm, b_k_norm):
    b, s, _ = x.shape
    m = mem.shape[1]
    cos_a, sin_a = axial_rope_tables(s, A_ROPE)
    cos_b, sin_b = axial_rope_tables(s, B_HD)
    for i in range(DEPTH):
        x = x + 0.5 * swiglu(rms_norm(x, ffn1_norm[i]), ffn1_w_gu[i], ffn1_w_down[i])
        u = rms_norm(x, mix_norm[i])
        j = i // N_MIXERS
        if i % N_MIXERS == 0:
            tok, q_mem = mla_mixer(u, a_w_in[j], a_q_a_norm[j], a_kv_a_norm[j], a_w_q_b[j], a_w_kv_b[j],
                                   a_q_nope_norm[j], a_q_pe_norm[j], a_k_nope_norm[j], a_k_pe_norm[j],
                                   cos_a, sin_a)
        else:
            tok, q_mem = gqa_mixer(u, b_w_in[j], b_q_norm[j], b_k_norm[j], cos_b, sin_b)
        mem_kv = (rms_norm(mem, mem_norm[i]) @ w_mem_kv[i]).reshape(b, m, 2, MEM_HEADS, MEM_HD)
        mk = rms_norm(mem_kv[:, :, 0], mem_k_norm[i])
        mv = mem_kv[:, :, 1]
        mq = rms_norm(q_mem.reshape(b, s, MEM_HEADS, MEM_HD), mem_q_norm[i])
        mo = memory_attention(mq, mk, mv)
        x = x + jnp.concatenate([tok, mo], axis=-1) @ w_o[i]
        x = x + 0.5 * swiglu(rms_norm(x, ffn2_norm[i]), ffn2_w_gu[i], ffn2_w_down[i])
    return x
```

```python
import functools

import jax
import jax.numpy as jnp
from jax import lax
from jax.experimental import pallas as pl
from jax.experimental.pallas import tpu as pltpu

EPS = 1e-6
GRID_W = 64
ROPE_THETA = 10000.0
HEAD_DIM = 128
A_ROPE = 64
MEM_HEADS = 4
N_MIXERS = 2
B_GROUP = 4
V7X_VMEM_BYTES = 64 * 1024 * 1024
VMEM_LIMIT = V7X_VMEM_BYTES * 7 // 8

F32 = jnp.float32
BF16 = jnp.bfloat16


def _params():
    return pltpu.CompilerParams(vmem_limit_bytes=VMEM_LIMIT)


def _rmsnorm_kernel(x_ref, g_ref, o_ref):
    x = x_ref[...]
    ms = jnp.mean(x * x, axis=-1, keepdims=True)
    o_ref[...] = (x * lax.rsqrt(ms + EPS) * g_ref[...]).astype(o_ref.dtype)


def rmsnorm(x, g, tm):
    m, d = x.shape
    return pl.pallas_call(
        _rmsnorm_kernel,
        grid=(m // tm,),
        in_specs=[pl.BlockSpec((tm, d), lambda i: (i, 0)),
                  pl.BlockSpec((1, d), lambda i: (0, 0))],
        out_specs=pl.BlockSpec((tm, d), lambda i: (i, 0)),
        out_shape=jax.ShapeDtypeStruct((m, d), BF16),
        compiler_params=_params(),
        name="rmsnorm",
    )(x, g.reshape(1, d))


def _mm_cast_kernel(a_ref, b_ref, o_ref):
    acc = jnp.dot(a_ref[...], b_ref[...], preferred_element_type=F32)
    o_ref[...] = acc.astype(o_ref.dtype)


def _mm_swiglu_kernel(a_ref, bg_ref, bu_ref, o_ref):
    a = a_ref[...]
    g = jnp.dot(a, bg_ref[...], preferred_element_type=F32)
    u = jnp.dot(a, bu_ref[...], preferred_element_type=F32)
    o_ref[...] = (g * (1.0 / (1.0 + jnp.exp(-g))) * u).astype(o_ref.dtype)


def _mm_residual_kernel(a_ref, b_ref, r_ref, o_ref, *, scale):
    acc = jnp.dot(a_ref[...], b_ref[...], preferred_element_type=F32)
    o_ref[...] = r_ref[...] + scale * acc


def _mm_headnorm_kernel(a_ref, b_ref, g_ref, *rest, cw, inv_count, shifts):
    o_ref = rest[-1]
    tabs = rest[:-1]
    acc = jnp.dot(a_ref[...], b_ref[...], preferred_element_type=F32)
    g = g_ref[...]
    for c in range(acc.shape[1] // cw):
        x = acc[:, c * cw:(c + 1) * cw]
        ms = jnp.sum(x * x, axis=-1, keepdims=True) * inv_count
        y = x * lax.rsqrt(ms + EPS) * g
        if shifts:
            out = y * tabs[0][...]
            for k, sh in enumerate(shifts):
                out = out + pltpu.roll(y, sh, 1) * tabs[1 + k][...]
            y = out
        o_ref[:, c * cw:(c + 1) * cw] = y.astype(o_ref.dtype)


def _mm_call(kernel, a, bs, extras, extra_specs, tm, tn, n_out, out_dtype, name):
    m, k = a.shape
    grid = (m // tm, n_out // tn)
    in_specs = [pl.BlockSpec((tm, k), lambda i, j: (i, 0))]
    args = [a]
    for b, off in bs:
        in_specs.append(pl.BlockSpec((k, tn), functools.partial(lambda i, j, o: (0, j + o), o=off // tn)))
        args.append(b)
    in_specs += extra_specs
    args += extras
    return pl.pallas_call(
        kernel,
        grid=grid,
        in_specs=in_specs,
        out_specs=pl.BlockSpec((tm, tn), lambda i, j: (i, j)),
        out_shape=jax.ShapeDtypeStruct((m, n_out), out_dtype),
        compiler_params=_params(),
        name=name,
    )(*args)


def mm_cast(a, b, tm, tn, out_dtype=BF16):
    return _mm_call(_mm_cast_kernel, a, [(b, 0)], [], [], tm, tn, b.shape[1], out_dtype, "mm_cast")


def mm_swiglu(a, w_gu, tm, tn):
    d_ff = w_gu.shape[1] // 2
    return _mm_call(_mm_swiglu_kernel, a, [(w_gu, 0), (w_gu, d_ff)], [], [], tm, tn, d_ff, BF16, "mm_swiglu")


def mm_residual(a, b, res, scale, tm, tn):
    spec = pl.BlockSpec((tm, tn), lambda i, j: (i, j))
    return _mm_call(functools.partial(_mm_residual_kernel, scale=scale), a, [(b, 0)], [res], [spec],
                    tm, tn, b.shape[1], F32, "mm_residual")


def mm_headnorm(a, b, g, cw, count, tm, tn, rope=None, seq_tiles=1):
    extras = [g.reshape(1, cw)]
    specs = [pl.BlockSpec((1, cw), lambda i, j: (0, 0))]
    shifts = ()
    if rope is not None:
        shifts, tabs = rope
        for t in tabs:
            extras.append(t)
            specs.append(pl.BlockSpec((tm, cw), lambda i, j: (i % seq_tiles, 0)))
    kern = functools.partial(_mm_headnorm_kernel, cw=cw, inv_count=1.0 / count, shifts=tuple(shifts))
    return _mm_call(kern, a, [(b, 0)], extras, specs, tm, tn, b.shape[1], BF16, "mm_headnorm")


def _attn_kernel(*refs, nparts, tk, nkv):
    q_refs = refs[:nparts]
    k_refs = refs[nparts:2 * nparts]
    v_ref = refs[2 * nparts]
    o_ref = refs[2 * nparts + 1]
    if nparts > 1:
        q = jnp.concatenate([r[...] for r in q_refs], axis=-1)
    else:
        q = q_refs[0][...]
    tq = q.shape[0]
    dv = v_ref.shape[-1]

    def body(t, carry):
        m, l, acc = carry
        off = pl.multiple_of(t * tk, tk)
        if nparts > 1:
            k = jnp.concatenate([r[pl.ds(off, tk), :] for r in k_refs], axis=-1)
        else:
            k = k_refs[0][pl.ds(off, tk), :]
        s = lax.dot_general(q, k, (((1,), (1,)), ((), ())), preferred_element_type=F32)
        m_new = jnp.maximum(m, jnp.max(s, axis=-1, keepdims=True))
        p = jnp.exp(s - m_new)
        alpha = jnp.exp(m - m_new)
        l = alpha * l + jnp.sum(p, axis=-1, keepdims=True)
        acc = alpha * acc + jnp.dot(p.astype(BF16), v_ref[pl.ds(off, tk), :], preferred_element_type=F32)
        return m_new, l, acc

    init = (jnp.full((tq, 1), -jnp.inf, F32), jnp.zeros((tq, 1), F32), jnp.zeros((tq, dv), F32))
    if nkv == 1:
        m, l, acc = body(0, init)
    else:
        m, l, acc = lax.fori_loop(0, nkv, body, init)
    o_ref[...] = (acc / l).astype(o_ref.dtype)


def attention(qs, ks, v, n_heads, group, dq, dv, tq, tk):
    b, s, _ = qs[0].shape
    skv = v.shape[1]
    nparts = len(qs)
    in_specs = [pl.BlockSpec((None, tq, dq), lambda bi, h, qi: (bi, qi, h)) for _ in qs]
    for k in ks:
        if k.shape[2] == dq:
            in_specs.append(pl.BlockSpec((None, skv, dq), lambda bi, h, qi: (bi, 0, 0)))
        else:
            in_specs.append(pl.BlockSpec((None, skv, dq), lambda bi, h, qi: (bi, 0, h // group)))
    in_specs.append(pl.BlockSpec((None, skv, dv), lambda bi, h, qi: (bi, 0, h // group)))
    kern = functools.partial(_attn_kernel, nparts=nparts, tk=tk, nkv=skv // tk)
    return pl.pallas_call(
        kern,
        grid=(b, n_heads, s // tq),
        in_specs=in_specs,
        out_specs=pl.BlockSpec((None, tq, dv), lambda bi, h, qi: (bi, qi, h)),
        out_shape=jax.ShapeDtypeStruct((b, s, n_heads * dv), BF16),
        compiler_params=_params(),
        name="attention",
    )(*qs, *ks, v)


def _deinterleave(w, width):
    lead = w.shape[:-1]
    n = w.shape[-1] // width
    w = w.reshape(lead + (n, width // 2, 2))
    w = jnp.swapaxes(w, -1, -2)
    return w.reshape(lead + (n * width,))


def _pad_groups(w, width, to):
    lead = w.shape[:-1]
    n = w.shape[-1] // width
    w = w.reshape(lead + (n, width))
    w = jnp.pad(w, [(0, 0)] * len(lead) + [(0, 0), (0, to - width)])
    return w.reshape(lead + (n * to,))


def _rope_angles(seq_len, rot_dim):
    rows = seq_len // GRID_W
    row, col = jnp.meshgrid(jnp.arange(rows), jnp.arange(GRID_W), indexing="ij")
    row = row.reshape(-1).astype(F32)
    col = col.reshape(-1).astype(F32)
    axis_dim = rot_dim // 2
    inv_freq = ROPE_THETA ** (-jnp.arange(0, axis_dim, 2, dtype=F32) / axis_dim)
    ang = jnp.concatenate([row[:, None] * inv_freq, col[:, None] * inv_freq], axis=-1)
    return jnp.cos(ang), jnp.sin(ang)


def _rope_tables(seq_len, rot_dim, width):
    cos, sin = _rope_angles(seq_len, rot_dim)
    half = rot_dim // 2
    z = jnp.zeros((seq_len, width - rot_dim), F32)
    zh = jnp.zeros_like(sin)
    c = jnp.concatenate([cos, cos, z], axis=-1)
    if rot_dim == width:
        return (half,), [c, jnp.concatenate([-sin, sin], axis=-1)]
    s_a = jnp.concatenate([-sin, zh, z], axis=-1)
    s_b = jnp.concatenate([zh, sin, z], axis=-1)
    return (width - half, half), [c, s_a, s_b]


def kernel(x, mem, ffn1_norm, ffn1_w_gu, ffn1_w_down, mix_norm, w_o, mem_norm, w_mem_kv, mem_q_norm, mem_k_norm, ffn2_norm, ffn2_w_gu, ffn2_w_down, a_w_in, a_q_a_norm, a_kv_a_norm, a_w_q_b, a_w_kv_b, a_q_nope_norm, a_q_pe_norm, a_k_nope_norm, a_k_pe_norm, b_w_in, b_q_norm, b_k_norm):
    b, s, d = x.shape
    n = b * s
    m_tok = mem.shape[1]
    depth = ffn1_norm.shape[0]
    hd = HEAD_DIM
    mem_w = w_mem_kv.shape[2] // 2
    mem_hd = mem_w // MEM_HEADS
    tok_w = w_o.shape[1] - mem_w
    n_heads = tok_w // hd
    q_lora = a_q_a_norm.shape[1]
    kv_lora = a_kv_a_norm.shape[1]
    n_kv = n_heads // B_GROUP

    tm = min(1024, s)
    seq_tiles = s // tm
    tq = min(1024, s)
    tk = min(512, s)
    tm_mem = min(512, b * m_tok)
    tm_norm = min(256, s)

    def tile(width, pref):
        t = min(pref, width)
        while width % t:
            t -= hd
        return t

    rope_a = _rope_tables(s, A_ROPE, hd)
    rope_b = _rope_tables(s, hd, hd)
    scale_a = float((hd + A_ROPE) ** -0.5)
    scale_b = float(hd ** -0.5)
    scale_m = float(mem_hd ** -0.5)

    xf = x.reshape(n, d)
    memf = mem.reshape(b * m_tok, d)

    def ffn(xf, norm, w_gu, w_down):
        xn = rmsnorm(xf, norm, tm_norm)
        act = mm_swiglu(xn, w_gu.astype(BF16), tm, tile(w_gu.shape[1] // 2, 512))
        return mm_residual(act, w_down.astype(BF16), xf, 0.5, tm, tile(d, 512))

    for i in range(depth):
        xf = ffn(xf, ffn1_norm[i], ffn1_w_gu[i], ffn1_w_down[i])
        u = rmsnorm(xf, mix_norm[i], tm_norm)
        j = i // N_MIXERS
        if i % N_MIXERS == 0:
            w_in = a_w_in[j]
            o1, o2, o3 = q_lora, q_lora + kv_lora, q_lora + kv_lora + A_ROPE
            w_cq = w_in[:, :o1].astype(BF16)
            w_ckv = w_in[:, o1:o2].astype(BF16)
            w_kpe = _pad_groups(_deinterleave(w_in[:, o2:o3], A_ROPE), A_ROPE, hd).astype(BF16)
            w_qm = w_in[:, o3:].astype(BF16)
            wq = a_w_q_b[j].reshape(q_lora, n_heads, hd + A_ROPE)
            w_qn = wq[:, :, :hd].reshape(q_lora, n_heads * hd).astype(BF16)
            w_qp = _pad_groups(_deinterleave(wq[:, :, hd:].reshape(q_lora, n_heads * A_ROPE), A_ROPE),
                               A_ROPE, hd).astype(BF16)
            wkv = a_w_kv_b[j].reshape(kv_lora, n_heads, 2 * hd)
            w_kn = wkv[:, :, :hd].reshape(kv_lora, n_heads * hd).astype(BF16)
            w_v = wkv[:, :, hd:].reshape(kv_lora, n_heads * hd).astype(BF16)
            g_qpe = _pad_groups(_deinterleave(a_q_pe_norm[j], A_ROPE), A_ROPE, hd) * scale_a
            g_kpe = _pad_groups(_deinterleave(a_k_pe_norm[j], A_ROPE), A_ROPE, hd)

            cqn = mm_headnorm(u, w_cq, a_q_a_norm[j], q_lora, q_lora, tm, q_lora)
            ckvn = mm_headnorm(u, w_ckv, a_kv_a_norm[j], kv_lora, kv_lora, tm, kv_lora)
            kpe = mm_headnorm(u, w_kpe, g_kpe, hd, A_ROPE, tm, hd, rope=rope_a, seq_tiles=seq_tiles)
            mq = mm_headnorm(u, w_qm, mem_q_norm[i] * scale_m, mem_hd, mem_hd, tm, tile(mem_w, 1024))
            tw = tile(n_heads * hd, 1024)
            qn = mm_headnorm(cqn, w_qn, a_q_nope_norm[j] * scale_a, hd, hd, tm, tw)
            qp = mm_headnorm(cqn, w_qp, g_qpe, hd, A_ROPE, tm, tw, rope=rope_a, seq_tiles=seq_tiles)
            kn = mm_headnorm(ckvn, w_kn, a_k_nope_norm[j], hd, hd, tm, tw)
            v = mm_cast(ckvn, w_v, tm, tw)
            r3 = lambda t: t.reshape(b, s, t.shape[-1])
            tok = attention([r3(qn), r3(qp)], [r3(kn), r3(kpe)], r3(v), n_heads, 1, hd, hd, tq, tk)
        else:
            w_in = b_w_in[j]
            qw, kw = n_heads * hd, n_kv * hd
            w_q = _deinterleave(w_in[:, :qw], hd).astype(BF16)
            w_k = _deinterleave(w_in[:, qw:qw + kw], hd).astype(BF16)
            w_v = w_in[:, qw + kw:qw + 2 * kw].astype(BF16)
            w_qm = w_in[:, qw + 2 * kw:].astype(BF16)
            g_q = _deinterleave(b_q_norm[j], hd) * scale_b
            g_k = _deinterleave(b_k_norm[j], hd)
            q = mm_headnorm(u, w_q, g_q, hd, hd, tm, tile(qw, 1024), rope=rope_b, seq_tiles=seq_tiles)
            k = mm_headnorm(u, w_k, g_k, hd, hd, tm, tile(kw, 1024), rope=rope_b, seq_tiles=seq_tiles)
            v = mm_cast(u, w_v, tm, tile(kw, 1024))
            mq = mm_headnorm(u, w_qm, mem_q_norm[i] * scale_m, mem_hd, mem_hd, tm, tile(mem_w, 1024))
            r3 = lambda t: t.reshape(b, s, t.shape[-1])
            tok = attention([r3(q)], [r3(k)], r3(v), n_heads, B_GROUP, hd, hd, tq, tk)

        memn = rmsnorm(memf, mem_norm[i], tm_mem)
        w_mkv = w_mem_kv[i].astype(BF16)
        mk = mm_headnorm(memn, w_mkv[:, :mem_w], mem_k_norm[i], mem_hd, mem_hd, tm_mem, tile(mem_w, 1024))
        mv = mm_cast(memn, w_mkv[:, mem_w:], tm_mem, tile(mem_w, 1024))
        mo = attention([mq.reshape(b, s, mem_w)], [mk.reshape(b, m_tok, mem_w)], mv.reshape(b, m_tok, mem_w),
                       MEM_HEADS, 1, mem_hd, mem_hd, tq, m_tok)
        mix = jnp.concatenate([tok.reshape(n, tok_w), mo.reshape(n, mem_w)], axis=-1)
        xf = mm_residual(mix, w_o[i].astype(BF16), xf, 1.0, tm, tile(d, 512))
        xf = ffn(xf, ffn2_norm[i], ffn2_w_gu[i], ffn2_w_down[i])
    return xf.reshape(b, s, d)
```

```python
import functools

import jax
import jax.numpy as jnp
from jax import lax
from jax.experimental import pallas as pl
from jax.experimental.pallas import tpu as pltpu

EPS = 1e-6
GRID_W = 64
ROPE_THETA = 10000.0
HEAD_DIM = 128
A_ROPE = 64
MEM_HEADS = 4
LOG2E = 1.4426950408889634
N_MIXERS = 2
B_GROUP = 4
V7X_VMEM_BYTES = 64 * 1024 * 1024
VMEM_LIMIT = V7X_VMEM_BYTES * 7 // 8

F32 = jnp.float32
BF16 = jnp.bfloat16


def _params():
    return pltpu.CompilerParams(vmem_limit_bytes=VMEM_LIMIT)


def _rmsnorm_kernel(x_ref, g_ref, o_ref):
    x = x_ref[...]
    ms = jnp.mean(x * x, axis=-1, keepdims=True)
    o_ref[...] = (x * lax.rsqrt(ms + EPS) * g_ref[...]).astype(o_ref.dtype)


def rmsnorm(x, g, tm):
    m, d = x.shape
    return pl.pallas_call(
        _rmsnorm_kernel,
        grid=(m // tm,),
        in_specs=[pl.BlockSpec((tm, d), lambda i: (i, 0)),
                  pl.BlockSpec((1, d), lambda i: (0, 0))],
        out_specs=pl.BlockSpec((tm, d), lambda i: (i, 0)),
        out_shape=jax.ShapeDtypeStruct((m, d), BF16),
        compiler_params=_params(),
        name="rmsnorm",
    )(x, g.reshape(1, d))


def _mm_cast_kernel(a_ref, b_ref, o_ref):
    acc = jnp.dot(a_ref[...], b_ref[...], preferred_element_type=F32)
    o_ref[...] = acc.astype(o_ref.dtype)


def _mm_swiglu_kernel(a_ref, bg_ref, bu_ref, o_ref):
    a = a_ref[...]
    g = jnp.dot(a, bg_ref[...], preferred_element_type=F32)
    u = jnp.dot(a, bu_ref[...], preferred_element_type=F32)
    o_ref[...] = (g * (1.0 / (1.0 + jnp.exp(-g))) * u).astype(o_ref.dtype)


def _mm_residual_kernel(a_ref, b_ref, r_ref, o_ref, *, scale):
    acc = jnp.dot(a_ref[...], b_ref[...], preferred_element_type=F32)
    o_ref[...] = r_ref[...] + scale * acc


def _mm_headnorm_kernel(a_ref, b_ref, g_ref, *rest, cw, inv_count, shifts):
    o_ref = rest[-1]
    tabs = rest[:-1]
    acc = jnp.dot(a_ref[...], b_ref[...], preferred_element_type=F32)
    g = g_ref[...]
    for c in range(acc.shape[1] // cw):
        x = acc[:, c * cw:(c + 1) * cw]
        ms = jnp.sum(x * x, axis=-1, keepdims=True) * inv_count
        y = x * lax.rsqrt(ms + EPS) * g
        if shifts:
            out = y * tabs[0][...]
            for k, sh in enumerate(shifts):
                out = out + pltpu.roll(y, sh, 1) * tabs[1 + k][...]
            y = out
        o_ref[:, c * cw:(c + 1) * cw] = y.astype(o_ref.dtype)


def _mm_call(kernel, a, bs, extras, extra_specs, tm, tn, n_out, out_dtype, name):
    m, k = a.shape
    grid = (m // tm, n_out // tn)
    in_specs = [pl.BlockSpec((tm, k), lambda i, j: (i, 0))]
    args = [a]
    for b, off in bs:
        in_specs.append(pl.BlockSpec((k, tn), functools.partial(lambda i, j, o: (0, j + o), o=off // tn)))
        args.append(b)
    in_specs += extra_specs
    args += extras
    return pl.pallas_call(
        kernel,
        grid=grid,
        in_specs=in_specs,
        out_specs=pl.BlockSpec((tm, tn), lambda i, j: (i, j)),
        out_shape=jax.ShapeDtypeStruct((m, n_out), out_dtype),
        compiler_params=_params(),
        name=name,
    )(*args)


def mm_cast(a, b, tm, tn, out_dtype=BF16):
    return _mm_call(_mm_cast_kernel, a, [(b, 0)], [], [], tm, tn, b.shape[1], out_dtype, "mm_cast")


def mm_swiglu(a, w_gu, tm, tn):
    d_ff = w_gu.shape[1] // 2
    return _mm_call(_mm_swiglu_kernel, a, [(w_gu, 0), (w_gu, d_ff)], [], [], tm, tn, d_ff, BF16, "mm_swiglu")


def mm_residual(a, b, res, scale, tm, tn):
    spec = pl.BlockSpec((tm, tn), lambda i, j: (i, j))
    return _mm_call(functools.partial(_mm_residual_kernel, scale=scale), a, [(b, 0)], [res], [spec],
                    tm, tn, b.shape[1], F32, "mm_residual")


def mm_headnorm(a, b, g, cw, count, tm, tn, rope=None, seq_tiles=1):
    extras = [g.reshape(1, cw)]
    specs = [pl.BlockSpec((1, cw), lambda i, j: (0, 0))]
    shifts = ()
    if rope is not None:
        shifts, tabs = rope
        for t in tabs:
            extras.append(t)
            specs.append(pl.BlockSpec((tm, cw), lambda i, j: (i % seq_tiles, 0)))
    kern = functools.partial(_mm_headnorm_kernel, cw=cw, inv_count=1.0 / count, shifts=tuple(shifts))
    return _mm_call(kern, a, [(b, 0)], extras, specs, tm, tn, b.shape[1], BF16, "mm_headnorm")


def _attn_kernel(*refs, nparts, tk, nkv):
    q_refs = refs[:nparts]
    k_refs = refs[nparts:2 * nparts]
    v_ref = refs[2 * nparts]
    o_ref = refs[2 * nparts + 1]
    if nparts > 1:
        q = jnp.concatenate([r[...] for r in q_refs], axis=-1)
    else:
        q = q_refs[0][...]
    tq = q.shape[0]
    dv = v_ref.shape[-1]

    def body(t, carry):
        m, l, acc = carry
        off = pl.multiple_of(t * tk, tk)
        if nparts > 1:
            k = jnp.concatenate([r[pl.ds(off, tk), :] for r in k_refs], axis=-1)
        else:
            k = k_refs[0][pl.ds(off, tk), :]
        s = lax.dot_general(q, k, (((1,), (1,)), ((), ())), preferred_element_type=F32)
        m_new = jnp.maximum(m, jnp.max(s, axis=-1, keepdims=True))
        p = jnp.exp(s - m_new)
        alpha = jnp.exp(m - m_new)
        l = alpha * l + jnp.sum(p, axis=-1, keepdims=True)
        acc = alpha * acc + jnp.dot(p.astype(BF16), v_ref[pl.ds(off, tk), :], preferred_element_type=F32)
        return m_new, l, acc

    init = (jnp.full((tq, 1), -jnp.inf, F32), jnp.zeros((tq, 1), F32), jnp.zeros((tq, dv), F32))
    if nkv == 1:
        m, l, acc = body(0, init)
    else:
        m, l, acc = lax.fori_loop(0, nkv, body, init)
    o_ref[...] = (acc / l).astype(o_ref.dtype)


def attention(qs, ks, v, n_heads, group, dq, dv, tq, tk):
    b, s, _ = qs[0].shape
    skv = v.shape[1]
    nparts = len(qs)
    in_specs = [pl.BlockSpec((None, tq, dq), lambda bi, h, qi: (bi, qi, h)) for _ in qs]
    for k in ks:
        if k.shape[2] == dq:
            in_specs.append(pl.BlockSpec((None, skv, dq), lambda bi, h, qi: (bi, 0, 0)))
        else:
            in_specs.append(pl.BlockSpec((None, skv, dq), lambda bi, h, qi: (bi, 0, h // group)))
    in_specs.append(pl.BlockSpec((None, skv, dv), lambda bi, h, qi: (bi, 0, h // group)))
    kern = functools.partial(_attn_kernel, nparts=nparts, tk=tk, nkv=skv // tk)
    return pl.pallas_call(
        kern,
        grid=(b, n_heads, s // tq),
        in_specs=in_specs,
        out_specs=pl.BlockSpec((None, tq, dv), lambda bi, h, qi: (bi, qi, h)),
        out_shape=jax.ShapeDtypeStruct((b, s, n_heads * dv), BF16),
        compiler_params=_params(),
        name="attention",
    )(*qs, *ks, v)


def _attn_t_kernel(*refs, nparts, tk, nkv):
    q_refs = refs[:nparts]
    k_refs = refs[nparts:2 * nparts]
    vt_ref = refs[2 * nparts]
    o_ref = refs[2 * nparts + 1]
    scratch = refs[2 * nparts + 2:]
    if nparts > 1:
        kcat_ref, s_ref, m_ref, l_ref, acc_ref = scratch
        q = jnp.concatenate([r[...] for r in q_refs], axis=-1)

        @pl.when(pl.program_id(2) == 0)
        def _():
            for i, r in enumerate(k_refs):
                kcat_ref[:, i * HEAD_DIM:(i + 1) * HEAD_DIM] = r[...]
        k_ref = kcat_ref
    else:
        s_ref, m_ref, l_ref, acc_ref = scratch
        q = q_refs[0][...]
        k_ref = k_refs[0]

    def qk(c, slot):
        off = pl.multiple_of(c * tk, tk)
        s_ref[slot] = lax.dot_general(k_ref[pl.ds(off, tk), :], q, (((1,), (1,)), ((), ())),
                                      preferred_element_type=F32)

    def softmax_pv(c, slot):
        s = s_ref[slot]
        m_old = m_ref[...]
        m_new = jnp.maximum(m_old, jnp.max(s, axis=0, keepdims=True))
        alpha = jnp.exp2(m_old - m_new)
        p = jnp.exp2(s - m_new)
        l_ref[...] = alpha * l_ref[...] + jnp.sum(p, axis=0, keepdims=True)
        m_ref[...] = m_new
        acc_ref[...] = alpha * acc_ref[...] + jnp.dot(vt_ref[c], p.astype(BF16), preferred_element_type=F32)

    m_ref[...] = jnp.full(m_ref.shape, -jnp.inf, F32)
    l_ref[...] = jnp.zeros(l_ref.shape, F32)
    acc_ref[...] = jnp.zeros(acc_ref.shape, F32)
    qk(0, 0)

    def pair(i, carry):
        c = 2 * i
        qk(c + 1, 1)
        softmax_pv(c, 0)
        qk(c + 2, 0)
        softmax_pv(c + 1, 1)
        return carry

    lax.fori_loop(0, nkv // 2 - 1, pair, 0)
    qk(nkv - 1, 1)
    softmax_pv(nkv - 2, 0)
    softmax_pv(nkv - 1, 1)
    o_ref[...] = (acc_ref[...] * (1.0 / l_ref[...])).T.astype(o_ref.dtype)


def attention_t(qs, ks, vt, n_heads, group, tq):
    b, s, _ = qs[0].shape
    skv = ks[0].shape[1]
    nkv, tk = vt.shape[1], vt.shape[3]
    n_kv = n_heads // group
    dv = vt.shape[2] // n_kv
    nparts = len(qs)
    assert nkv % 2 == 0 and nkv * tk == skv
    in_specs = [pl.BlockSpec((None, tq, HEAD_DIM), lambda bi, h, qi: (bi, qi, h)) for _ in qs]
    for k in ks:
        if k.shape[2] == n_kv * HEAD_DIM:
            in_specs.append(pl.BlockSpec((None, skv, HEAD_DIM), lambda bi, h, qi: (bi, 0, h // group)))
        else:
            in_specs.append(pl.BlockSpec((None, skv, HEAD_DIM), lambda bi, h, qi: (bi, 0, 0)))
    in_specs.append(pl.BlockSpec((None, nkv, dv, tk), lambda bi, h, qi: (bi, 0, h // group, 0)))
    scratch = []
    if nparts > 1:
        scratch.append(pltpu.VMEM((skv, nparts * HEAD_DIM), BF16))
    scratch += [pltpu.VMEM((2, tk, tq), F32), pltpu.VMEM((1, tq), F32), pltpu.VMEM((1, tq), F32),
                pltpu.VMEM((dv, tq), F32)]
    kern = functools.partial(_attn_t_kernel, nparts=nparts, tk=tk, nkv=nkv)
    return pl.pallas_call(
        kern,
        grid=(b, n_heads, s // tq),
        in_specs=in_specs,
        out_specs=pl.BlockSpec((None, tq, dv), lambda bi, h, qi: (bi, qi, h)),
        out_shape=jax.ShapeDtypeStruct((b, s, n_heads * dv), BF16),
        scratch_shapes=scratch,
        compiler_params=_params(),
        name="attention_t",
    )(*qs, *ks, vt)


def _mm_vt_kernel(a_ref, wt_ref, o_ref):
    o_ref[...] = lax.dot_general(wt_ref[...], a_ref[...], (((1,), (1,)), ((), ())),
                                 preferred_element_type=F32).astype(o_ref.dtype)


def mm_vt(a, wt, tk, tn):
    m, k = a.shape
    cols = wt.shape[0]
    return pl.pallas_call(
        _mm_vt_kernel,
        grid=(m // tk, cols // tn),
        in_specs=[pl.BlockSpec((tk, k), lambda i, j: (i, 0)),
                  pl.BlockSpec((tn, k), lambda i, j: (j, 0))],
        out_specs=pl.BlockSpec((None, tn, tk), lambda i, j: (i, j, 0)),
        out_shape=jax.ShapeDtypeStruct((m // tk, cols, tk), BF16),
        compiler_params=_params(),
        name="mm_vt",
    )(a, wt)


def _mm_residual2_kernel(a1_ref, a2_ref, b1_ref, b2_ref, r_ref, o_ref):
    acc = jnp.dot(a1_ref[...], b1_ref[...], preferred_element_type=F32)
    acc = acc + jnp.dot(a2_ref[...], b2_ref[...], preferred_element_type=F32)
    o_ref[...] = r_ref[...] + acc


def mm_residual2(a1, a2, w, res, tm, tn):
    m, k1 = a1.shape
    k2 = a2.shape[1]
    n_out = w.shape[1]
    assert k1 % k2 == 0
    return pl.pallas_call(
        _mm_residual2_kernel,
        grid=(m // tm, n_out // tn),
        in_specs=[pl.BlockSpec((tm, k1), lambda i, j: (i, 0)),
                  pl.BlockSpec((tm, k2), lambda i, j: (i, 0)),
                  pl.BlockSpec((k1, tn), lambda i, j: (0, j)),
                  pl.BlockSpec((k2, tn), lambda i, j: (k1 // k2, j)),
                  pl.BlockSpec((tm, tn), lambda i, j: (i, j))],
        out_specs=pl.BlockSpec((tm, tn), lambda i, j: (i, j)),
        out_shape=jax.ShapeDtypeStruct((m, n_out), F32),
        compiler_params=_params(),
        name="mm_residual2",
    )(a1, a2, w, w, res)


def _deinterleave(w, width):
    lead = w.shape[:-1]
    n = w.shape[-1] // width
    w = w.reshape(lead + (n, width // 2, 2))
    w = jnp.swapaxes(w, -1, -2)
    return w.reshape(lead + (n * width,))


def _pad_groups(w, width, to):
    lead = w.shape[:-1]
    n = w.shape[-1] // width
    w = w.reshape(lead + (n, width))
    w = jnp.pad(w, [(0, 0)] * len(lead) + [(0, 0), (0, to - width)])
    return w.reshape(lead + (n * to,))


def _rope_angles(seq_len, rot_dim):
    rows = seq_len // GRID_W
    row, col = jnp.meshgrid(jnp.arange(rows), jnp.arange(GRID_W), indexing="ij")
    row = row.reshape(-1).astype(F32)
    col = col.reshape(-1).astype(F32)
    axis_dim = rot_dim // 2
    inv_freq = ROPE_THETA ** (-jnp.arange(0, axis_dim, 2, dtype=F32) / axis_dim)
    ang = jnp.concatenate([row[:, None] * inv_freq, col[:, None] * inv_freq], axis=-1)
    return jnp.cos(ang), jnp.sin(ang)


def _rope_tables(seq_len, rot_dim, width):
    cos, sin = _rope_angles(seq_len, rot_dim)
    half = rot_dim // 2
    z = jnp.zeros((seq_len, width - rot_dim), F32)
    zh = jnp.zeros_like(sin)
    c = jnp.concatenate([cos, cos, z], axis=-1)
    if rot_dim == width:
        return (half,), [c, jnp.concatenate([-sin, sin], axis=-1)]
    s_a = jnp.concatenate([-sin, zh, z], axis=-1)
    s_b = jnp.concatenate([zh, sin, z], axis=-1)
    return (width - half, half), [c, s_a, s_b]


def kernel(x, mem, ffn1_norm, ffn1_w_gu, ffn1_w_down, mix_norm, w_o, mem_norm, w_mem_kv, mem_q_norm, mem_k_norm, ffn2_norm, ffn2_w_gu, ffn2_w_down, a_w_in, a_q_a_norm, a_kv_a_norm, a_w_q_b, a_w_kv_b, a_q_nope_norm, a_q_pe_norm, a_k_nope_norm, a_k_pe_norm, b_w_in, b_q_norm, b_k_norm):
    b, s, d = x.shape
    n = b * s
    m_tok = mem.shape[1]
    depth = ffn1_norm.shape[0]
    hd = HEAD_DIM
    mem_w = w_mem_kv.shape[2] // 2
    mem_hd = mem_w // MEM_HEADS
    tok_w = w_o.shape[1] - mem_w
    n_heads = tok_w // hd
    q_lora = a_q_a_norm.shape[1]
    kv_lora = a_kv_a_norm.shape[1]
    n_kv = n_heads // B_GROUP

    tm = min(1024, s)
    seq_tiles = s // tm
    tq = min(1024, s)
    tk = min(512, s // 2)
    tm_mem = min(512, b * m_tok)
    tm_norm = min(256, s)

    def tile(width, pref):
        t = min(pref, width)
        while width % t:
            t -= hd
        return t

    rope_a = _rope_tables(s, A_ROPE, hd)
    rope_b = _rope_tables(s, hd, hd)
    scale_a = float((hd + A_ROPE) ** -0.5) * LOG2E
    scale_b = float(hd ** -0.5) * LOG2E
    scale_m = float(mem_hd ** -0.5)

    xf = x.reshape(n, d)
    memf = mem.reshape(b * m_tok, d)

    def ffn(xf, norm, w_gu, w_down):
        xn = rmsnorm(xf, norm, tm_norm)
        act = mm_swiglu(xn, w_gu.astype(BF16), tm, tile(w_gu.shape[1] // 2, 512))
        return mm_residual(act, w_down.astype(BF16), xf, 0.5, tm, tile(d, 512))

    for i in range(depth):
        xf = ffn(xf, ffn1_norm[i], ffn1_w_gu[i], ffn1_w_down[i])
        u = rmsnorm(xf, mix_norm[i], tm_norm)
        j = i // N_MIXERS
        if i % N_MIXERS == 0:
            w_in = a_w_in[j]
            o1, o2, o3 = q_lora, q_lora + kv_lora, q_lora + kv_lora + A_ROPE
            w_cq = w_in[:, :o1].astype(BF16)
            w_ckv = w_in[:, o1:o2].astype(BF16)
            w_kpe = _pad_groups(_deinterleave(w_in[:, o2:o3], A_ROPE), A_ROPE, hd).astype(BF16)
            w_qm = w_in[:, o3:].astype(BF16)
            wq = a_w_q_b[j].reshape(q_lora, n_heads, hd + A_ROPE)
            w_qn = wq[:, :, :hd].reshape(q_lora, n_heads * hd).astype(BF16)
            w_qp = _pad_groups(_deinterleave(wq[:, :, hd:].reshape(q_lora, n_heads * A_ROPE), A_ROPE),
                               A_ROPE, hd).astype(BF16)
            wkv = a_w_kv_b[j].reshape(kv_lora, n_heads, 2 * hd)
            w_kn = wkv[:, :, :hd].reshape(kv_lora, n_heads * hd).astype(BF16)
            w_v = wkv[:, :, hd:].reshape(kv_lora, n_heads * hd).astype(BF16)
            g_qpe = _pad_groups(_deinterleave(a_q_pe_norm[j], A_ROPE), A_ROPE, hd) * scale_a
            g_kpe = _pad_groups(_deinterleave(a_k_pe_norm[j], A_ROPE), A_ROPE, hd)

            cqn = mm_headnorm(u, w_cq, a_q_a_norm[j], q_lora, q_lora, tm, q_lora)
            ckvn = mm_headnorm(u, w_ckv, a_kv_a_norm[j], kv_lora, kv_lora, tm, kv_lora)
            kpe = mm_headnorm(u, w_kpe, g_kpe, hd, A_ROPE, tm, hd, rope=rope_a, seq_tiles=seq_tiles)
            mq = mm_headnorm(u, w_qm, mem_q_norm[i] * scale_m, mem_hd, mem_hd, tm, tile(mem_w, 1024))
            tw = tile(n_heads * hd, 1024)
            qn = mm_headnorm(cqn, w_qn, a_q_nope_norm[j] * scale_a, hd, hd, tm, tw)
            qp = mm_headnorm(cqn, w_qp, g_qpe, hd, A_ROPE, tm, tw, rope=rope_a, seq_tiles=seq_tiles)
            kn = mm_headnorm(ckvn, w_kn, a_k_nope_norm[j], hd, hd, tm, tw)
            vt = mm_vt(ckvn, w_v.T, tk, tw).reshape(b, s // tk, n_heads * hd, tk)
            r3 = lambda t: t.reshape(b, s, t.shape[-1])
            tok = attention_t([r3(qn), r3(qp)], [r3(kn), r3(kpe)], vt, n_heads, 1, tq)
        else:
            w_in = b_w_in[j]
            qw, kw = n_heads * hd, n_kv * hd
            w_q = _deinterleave(w_in[:, :qw], hd).astype(BF16)
            w_k = _deinterleave(w_in[:, qw:qw + kw], hd).astype(BF16)
            w_v = w_in[:, qw + kw:qw + 2 * kw].astype(BF16)
            w_qm = w_in[:, qw + 2 * kw:].astype(BF16)
            g_q = _deinterleave(b_q_norm[j], hd) * scale_b
            g_k = _deinterleave(b_k_norm[j], hd)
            q = mm_headnorm(u, w_q, g_q, hd, hd, tm, tile(qw, 1024), rope=rope_b, seq_tiles=seq_tiles)
            k = mm_headnorm(u, w_k, g_k, hd, hd, tm, tile(kw, 1024), rope=rope_b, seq_tiles=seq_tiles)
            vt = mm_vt(u, w_v.T, tk, tile(kw, 1024)).reshape(b, s // tk, kw, tk)
            mq = mm_headnorm(u, w_qm, mem_q_norm[i] * scale_m, mem_hd, mem_hd, tm, tile(mem_w, 1024))
            r3 = lambda t: t.reshape(b, s, t.shape[-1])
            tok = attention_t([r3(q)], [r3(k)], vt, n_heads, B_GROUP, tq)

        memn = rmsnorm(memf, mem_norm[i], tm_mem)
        w_mkv = w_mem_kv[i].astype(BF16)
        mk = mm_headnorm(memn, w_mkv[:, :mem_w], mem_k_norm[i], mem_hd, mem_hd, tm_mem, tile(mem_w, 1024))
        mv = mm_cast(memn, w_mkv[:, mem_w:], tm_mem, tile(mem_w, 1024))
        mo = attention([mq.reshape(b, s, mem_w)], [mk.reshape(b, m_tok, mem_w)], mv.reshape(b, m_tok, mem_w),
                       MEM_HEADS, 1, mem_hd, mem_hd, tq, m_tok)
        xf = mm_residual2(tok.reshape(n, tok_w), mo.reshape(n, mem_w), w_o[i].astype(BF16), xf, tm, tile(d, 512))
        xf = ffn(xf, ffn2_norm[i], ffn2_w_gu[i], ffn2_w_down[i])
    return xf.reshape(b, s, d)
```

```python
import functools

import jax
import jax.numpy as jnp
from jax import lax
from jax.experimental import pallas as pl
from jax.experimental.pallas import tpu as pltpu

EPS = 1e-6
GRID_W = 64
ROPE_THETA = 10000.0
LANES = 128
HEAD_DIM = LANES
A_ROPE = 64
MEM_HEADS = 4
LOG2E = 1.4426950408889634
N_MIXERS = 2
B_GROUP = 4
V7X_VMEM_BYTES = 64 * 1024 * 1024
VMEM_LIMIT = V7X_VMEM_BYTES * 7 // 8

F32 = jnp.float32
BF16 = jnp.bfloat16


def _params():
    return pltpu.CompilerParams(vmem_limit_bytes=VMEM_LIMIT)


def _rmsnorm_kernel(x_ref, g_ref, o_ref):
    x = x_ref[...]
    ms = jnp.mean(x * x, axis=-1, keepdims=True)
    o_ref[...] = (x * lax.rsqrt(ms + EPS) * g_ref[...]).astype(o_ref.dtype)


def rmsnorm(x, g, tm):
    m, d = x.shape
    return pl.pallas_call(
        _rmsnorm_kernel,
        grid=(m // tm,),
        in_specs=[pl.BlockSpec((tm, d), lambda i: (i, 0)),
                  pl.BlockSpec((1, d), lambda i: (0, 0))],
        out_specs=pl.BlockSpec((tm, d), lambda i: (i, 0)),
        out_shape=jax.ShapeDtypeStruct((m, d), BF16),
        compiler_params=_params(),
        name="rmsnorm",
    )(x, g.reshape(1, d))


def _row_scale_kernel(ss_ref, r_ref, *, inv_d):
    ss = ss_ref[...]
    tot = ss[:, :LANES]
    for c in range(1, ss.shape[1] // LANES):
        tot = tot + ss[:, c * LANES:(c + 1) * LANES]
    r_ref[...] = lax.rsqrt(tot * inv_d + EPS)


def row_scale(ss, d_norm, tm):
    m, w = ss.shape
    return pl.pallas_call(
        functools.partial(_row_scale_kernel, inv_d=1.0 / d_norm),
        grid=(m // tm,),
        in_specs=[pl.BlockSpec((tm, w), lambda i: (i, 0))],
        out_specs=pl.BlockSpec((tm, LANES), lambda i: (i, 0)),
        out_shape=jax.ShapeDtypeStruct((m, LANES), F32),
        compiler_params=_params(),
        name="row_scale",
    )(ss)


def _widen(r, width):
    return r if width == LANES else jnp.tile(r, (1, width // LANES))


def _scale_prep_kernel(x_ref, g_ref, xg_ref, r_ref):
    x = x_ref[...]
    xg_ref[...] = (x * g_ref[...]).astype(BF16)
    r = lax.rsqrt(jnp.mean(x * x, axis=-1, keepdims=True) + EPS)
    r_ref[...] = jnp.broadcast_to(r, r_ref.shape)


def scale_prep(x, g, tm):
    m, d = x.shape
    return pl.pallas_call(
        _scale_prep_kernel,
        grid=(m // tm,),
        in_specs=[pl.BlockSpec((tm, d), lambda i: (i, 0)),
                  pl.BlockSpec((1, d), lambda i: (0, 0))],
        out_specs=[pl.BlockSpec((tm, d), lambda i: (i, 0)),
                   pl.BlockSpec((tm, LANES), lambda i: (i, 0))],
        out_shape=[jax.ShapeDtypeStruct((m, d), BF16), jax.ShapeDtypeStruct((m, LANES), F32)],
        compiler_params=_params(),
        name="scale_prep",
    )(x, g.reshape(1, d))


def _mm_cast_kernel(a_ref, b_ref, o_ref):
    acc = jnp.dot(a_ref[...], b_ref[...], preferred_element_type=F32)
    o_ref[...] = acc.astype(o_ref.dtype)


def _mm_swiglu_kernel(a_ref, r_ref, bg_ref, bu_ref, o_ref):
    a = a_ref[...]
    r = _widen(r_ref[...], o_ref.shape[1])
    g = jnp.dot(a, bg_ref[...], preferred_element_type=F32) * r
    u = jnp.dot(a, bu_ref[...], preferred_element_type=F32) * r
    o_ref[...] = (g * (1.0 / (1.0 + jnp.exp(-g))) * u).astype(o_ref.dtype)


def _mm_residual_kernel(*refs, n_a, scale, emit_next):
    a_refs = refs[:n_a]
    b_refs = refs[n_a:2 * n_a]
    r_ref = refs[2 * n_a]
    acc = jnp.dot(a_refs[0][...], b_refs[0][...], preferred_element_type=F32)
    for a_ref, b_ref in zip(a_refs[1:], b_refs[1:]):
        acc = acc + jnp.dot(a_ref[...], b_ref[...], preferred_element_type=F32)
    x = r_ref[...] + (acc if scale == 1.0 else scale * acc)
    if emit_next:
        g_ref = refs[2 * n_a + 1]
        o_ref, xg_ref, ss_ref = refs[-3:]
        xg_ref[...] = (x * g_ref[...]).astype(BF16)
        ss_ref[...] = jnp.broadcast_to(jnp.sum(x * x, axis=-1, keepdims=True), ss_ref.shape)
    else:
        o_ref = refs[-1]
    o_ref[...] = x


def _mm_headnorm_kernel(*refs, cw, inv_count, shifts, scaled):
    a_ref, b_ref = refs[0], refs[1]
    nxt = 2
    acc = jnp.dot(a_ref[...], b_ref[...], preferred_element_type=F32)
    if scaled:
        acc = acc * _widen(refs[2][...], acc.shape[1])
        nxt = 3
    g = refs[nxt][...]
    tabs = refs[nxt + 1:-1]
    o_ref = refs[-1]
    for c in range(acc.shape[1] // cw):
        x = acc[:, c * cw:(c + 1) * cw]
        ms = jnp.sum(x * x, axis=-1, keepdims=True) * inv_count
        y = x * lax.rsqrt(ms + EPS) * g
        if shifts:
            out = y * tabs[0][...]
            for k, sh in enumerate(shifts):
                out = out + pltpu.roll(y, sh, 1) * tabs[1 + k][...]
            y = out
        o_ref[:, c * cw:(c + 1) * cw] = y.astype(o_ref.dtype)


def _r_spec(tm):
    return pl.BlockSpec((tm, LANES), lambda i, j: (i, 0))


def mm_cast(a, b, tm, tn, out_dtype=BF16):
    m, k = a.shape
    n_out = b.shape[1]
    return pl.pallas_call(
        _mm_cast_kernel,
        grid=(m // tm, n_out // tn),
        in_specs=[pl.BlockSpec((tm, k), lambda i, j: (i, 0)),
                  pl.BlockSpec((k, tn), lambda i, j: (0, j))],
        out_specs=pl.BlockSpec((tm, tn), lambda i, j: (i, j)),
        out_shape=jax.ShapeDtypeStruct((m, n_out), out_dtype),
        compiler_params=_params(),
        name="mm_cast",
    )(a, b)


def mm_swiglu(xg, r, w_gu, layer, tm, tn):
    m, k = xg.shape
    d_ff = w_gu.shape[2] // 2
    off = d_ff // tn
    return pl.pallas_call(
        _mm_swiglu_kernel,
        grid=(m // tm, d_ff // tn),
        in_specs=[pl.BlockSpec((tm, k), lambda i, j: (i, 0)),
                  _r_spec(tm),
                  pl.BlockSpec((None, k, tn), lambda i, j: (layer, 0, j)),
                  pl.BlockSpec((None, k, tn), lambda i, j: (layer, 0, j + off))],
        out_specs=pl.BlockSpec((tm, tn), lambda i, j: (i, j)),
        out_shape=jax.ShapeDtypeStruct((m, d_ff), BF16),
        compiler_params=_params(),
        name="mm_swiglu",
    )(xg, r, w_gu, w_gu)


def mm_residual(a_list, w, layer, res, scale, g_next, tm, tn):
    m = a_list[0].shape[0]
    n_out = w.shape[2]
    in_specs = [pl.BlockSpec((tm, a.shape[1]), lambda i, j: (i, 0)) for a in a_list]
    row = 0
    for a in a_list:
        ka = a.shape[1]
        assert row % ka == 0
        in_specs.append(pl.BlockSpec((None, ka, tn), functools.partial(lambda i, j, rb: (layer, rb, j), rb=row // ka)))
        row += ka
    assert row == w.shape[1]
    tile_spec = pl.BlockSpec((tm, tn), lambda i, j: (i, j))
    in_specs.append(tile_spec)
    args = list(a_list) + [w] * len(a_list) + [res]
    emit_next = g_next is not None
    if emit_next:
        in_specs.append(pl.BlockSpec((1, tn), lambda i, j: (0, j)))
        args.append(g_next.reshape(1, n_out))
        out_specs = [tile_spec, tile_spec, pl.BlockSpec((tm, LANES), lambda i, j: (i, j))]
        out_shape = [jax.ShapeDtypeStruct((m, n_out), F32), jax.ShapeDtypeStruct((m, n_out), BF16),
                     jax.ShapeDtypeStruct((m, (n_out // tn) * LANES), F32)]
    else:
        out_specs = tile_spec
        out_shape = jax.ShapeDtypeStruct((m, n_out), F32)
    return pl.pallas_call(
        functools.partial(_mm_residual_kernel, n_a=len(a_list), scale=scale, emit_next=emit_next),
        grid=(m // tm, n_out // tn),
        in_specs=in_specs,
        out_specs=out_specs,
        out_shape=out_shape,
        compiler_params=_params(),
        name="mm_residual",
    )(*args)


def mm_headnorm(a, b, g, cw, count, tm, tn, rope=None, seq_tiles=1, r=None):
    m, k = a.shape
    n_out = b.shape[1]
    in_specs = [pl.BlockSpec((tm, k), lambda i, j: (i, 0)),
                pl.BlockSpec((k, tn), lambda i, j: (0, j))]
    args = [a, b]
    if r is not None:
        in_specs.append(_r_spec(tm))
        args.append(r)
    in_specs.append(pl.BlockSpec((1, cw), lambda i, j: (0, 0)))
    args.append(g.reshape(1, cw))
    shifts = ()
    if rope is not None:
        shifts, tabs = rope
        for t in tabs:
            in_specs.append(pl.BlockSpec((tm, cw), lambda i, j: (i % seq_tiles, 0)))
            args.append(t)
    kern = functools.partial(_mm_headnorm_kernel, cw=cw, inv_count=1.0 / count, shifts=tuple(shifts),
                             scaled=r is not None)
    return pl.pallas_call(
        kern,
        grid=(m // tm, n_out // tn),
        in_specs=in_specs,
        out_specs=pl.BlockSpec((tm, tn), lambda i, j: (i, j)),
        out_shape=jax.ShapeDtypeStruct((m, n_out), BF16),
        compiler_params=_params(),
        name="mm_headnorm",
    )(*args)


def _attn_kernel(*refs, nparts, tk, nkv):
    q_refs = refs[:nparts]
    k_refs = refs[nparts:2 * nparts]
    v_ref = refs[2 * nparts]
    o_ref = refs[2 * nparts + 1]
    if nparts > 1:
        q = jnp.concatenate([r[...] for r in q_refs], axis=-1)
    else:
        q = q_refs[0][...]
    tq = q.shape[0]
    dv = v_ref.shape[-1]

    def body(t, carry):
        m, l, acc = carry
        off = pl.multiple_of(t * tk, tk)
        if nparts > 1:
            k = jnp.concatenate([r[pl.ds(off, tk), :] for r in k_refs], axis=-1)
        else:
            k = k_refs[0][pl.ds(off, tk), :]
        s = lax.dot_general(q, k, (((1,), (1,)), ((), ())), preferred_element_type=F32)
        m_new = jnp.maximum(m, jnp.max(s, axis=-1, keepdims=True))
        p = jnp.exp(s - m_new)
        alpha = jnp.exp(m - m_new)
        l = alpha * l + jnp.sum(p, axis=-1, keepdims=True)
        acc = alpha * acc + jnp.dot(p.astype(BF16), v_ref[pl.ds(off, tk), :], preferred_element_type=F32)
        return m_new, l, acc

    init = (jnp.full((tq, 1), -jnp.inf, F32), jnp.zeros((tq, 1), F32), jnp.zeros((tq, dv), F32))
    if nkv == 1:
        m, l, acc = body(0, init)
    else:
        m, l, acc = lax.fori_loop(0, nkv, body, init)
    o_ref[...] = (acc / l).astype(o_ref.dtype)


def attention(qs, ks, v, n_heads, group, dq, dv, tq, tk):
    b, s, _ = qs[0].shape
    skv = v.shape[1]
    nparts = len(qs)
    in_specs = [pl.BlockSpec((None, tq, dq), lambda bi, h, qi: (bi, qi, h)) for _ in qs]
    for k in ks:
        if k.shape[2] == dq:
            in_specs.append(pl.BlockSpec((None, skv, dq), lambda bi, h, qi: (bi, 0, 0)))
        else:
            in_specs.append(pl.BlockSpec((None, skv, dq), lambda bi, h, qi: (bi, 0, h // group)))
    in_specs.append(pl.BlockSpec((None, skv, dv), lambda bi, h, qi: (bi, 0, h // group)))
    kern = functools.partial(_attn_kernel, nparts=nparts, tk=tk, nkv=skv // tk)
    return pl.pallas_call(
        kern,
        grid=(b, n_heads, s // tq),
        in_specs=in_specs,
        out_specs=pl.BlockSpec((None, tq, dv), lambda bi, h, qi: (bi, qi, h)),
        out_shape=jax.ShapeDtypeStruct((b, s, n_heads * dv), BF16),
        compiler_params=_params(),
        name="attention",
    )(*qs, *ks, v)


def _attn_t_kernel(*refs, nparts, tk, nkv):
    q_refs = refs[:nparts]
    k_refs = refs[nparts:2 * nparts]
    vt_ref = refs[2 * nparts]
    o_ref = refs[2 * nparts + 1]
    scratch = refs[2 * nparts + 2:]
    if nparts > 1:
        kcat_ref, s_ref, m_ref, l_ref, acc_ref = scratch
        q = jnp.concatenate([r[...] for r in q_refs], axis=-1)

        @pl.when(pl.program_id(2) == 0)
        def _():
            for i, r in enumerate(k_refs):
                kcat_ref[:, i * HEAD_DIM:(i + 1) * HEAD_DIM] = r[...]
        k_ref = kcat_ref
    else:
        s_ref, m_ref, l_ref, acc_ref = scratch
        q = q_refs[0][...]
        k_ref = k_refs[0]

    def qk(c, slot):
        off = pl.multiple_of(c * tk, tk)
        s_ref[slot] = lax.dot_general(k_ref[pl.ds(off, tk), :], q, (((1,), (1,)), ((), ())),
                                      preferred_element_type=F32)

    def softmax_pv(c, slot):
        s = s_ref[slot]
        m_old = m_ref[...]
        m_new = jnp.maximum(m_old, jnp.max(s, axis=0, keepdims=True))
        alpha = jnp.exp2(m_old - m_new)
        p = jnp.exp2(s - m_new)
        l_ref[...] = alpha * l_ref[...] + jnp.sum(p, axis=0, keepdims=True)
        m_ref[...] = m_new
        acc_ref[...] = alpha * acc_ref[...] + jnp.dot(vt_ref[c], p.astype(BF16), preferred_element_type=F32)

    m_ref[...] = jnp.full(m_ref.shape, -jnp.inf, F32)
    l_ref[...] = jnp.zeros(l_ref.shape, F32)
    acc_ref[...] = jnp.zeros(acc_ref.shape, F32)
    qk(0, 0)

    def pair(i, carry):
        c = 2 * i
        qk(c + 1, 1)
        softmax_pv(c, 0)
        qk(c + 2, 0)
        softmax_pv(c + 1, 1)
        return carry

    lax.fori_loop(0, nkv // 2 - 1, pair, 0)
    qk(nkv - 1, 1)
    softmax_pv(nkv - 2, 0)
    softmax_pv(nkv - 1, 1)
    o_ref[...] = (acc_ref[...] * (1.0 / l_ref[...])).T.astype(o_ref.dtype)


def attention_t(qs, ks, vt, n_heads, group, tq):
    b, s, _ = qs[0].shape
    skv = ks[0].shape[1]
    nkv, tk = vt.shape[1], vt.shape[3]
    n_kv = n_heads // group
    dv = vt.shape[2] // n_kv
    nparts = len(qs)
    assert nkv % 2 == 0 and nkv * tk == skv
    in_specs = [pl.BlockSpec((None, tq, HEAD_DIM), lambda bi, h, qi: (bi, qi, h)) for _ in qs]
    for k in ks:
        if k.shape[2] == n_kv * HEAD_DIM:
            in_specs.append(pl.BlockSpec((None, skv, HEAD_DIM), lambda bi, h, qi: (bi, 0, h // group)))
        else:
            in_specs.append(pl.BlockSpec((None, skv, HEAD_DIM), lambda bi, h, qi: (bi, 0, 0)))
    in_specs.append(pl.BlockSpec((None, nkv, dv, tk), lambda bi, h, qi: (bi, 0, h // group, 0)))
    scratch = []
    if nparts > 1:
        scratch.append(pltpu.VMEM((skv, nparts * HEAD_DIM), BF16))
    scratch += [pltpu.VMEM((2, tk, tq), F32), pltpu.VMEM((1, tq), F32), pltpu.VMEM((1, tq), F32),
                pltpu.VMEM((dv, tq), F32)]
    kern = functools.partial(_attn_t_kernel, nparts=nparts, tk=tk, nkv=nkv)
    return pl.pallas_call(
        kern,
        grid=(b, n_heads, s // tq),
        in_specs=in_specs,
        out_specs=pl.BlockSpec((None, tq, dv), lambda bi, h, qi: (bi, qi, h)),
        out_shape=jax.ShapeDtypeStruct((b, s, n_heads * dv), BF16),
        scratch_shapes=scratch,
        compiler_params=_params(),
        name="attention_t",
    )(*qs, *ks, vt)


def _mm_vt_kernel(*refs, scaled):
    a_ref, wt_ref, o_ref = refs[0], refs[1], refs[-1]
    out = lax.dot_general(wt_ref[...], a_ref[...], (((1,), (1,)), ((), ())), preferred_element_type=F32)
    if scaled:
        out = out * refs[2][...].T[:1, :]
    o_ref[...] = out.astype(o_ref.dtype)


def mm_vt(a, wt, tk, tn, r=None):
    m, k = a.shape
    cols = wt.shape[0]
    in_specs = [pl.BlockSpec((tk, k), lambda i, j: (i, 0)),
                pl.BlockSpec((tn, k), lambda i, j: (j, 0))]
    args = [a, wt]
    if r is not None:
        in_specs.append(_r_spec(tk))
        args.append(r)
    return pl.pallas_call(
        functools.partial(_mm_vt_kernel, scaled=r is not None),
        grid=(m // tk, cols // tn),
        in_specs=in_specs,
        out_specs=pl.BlockSpec((None, tn, tk), lambda i, j: (i, j, 0)),
        out_shape=jax.ShapeDtypeStruct((m // tk, cols, tk), BF16),
        compiler_params=_params(),
        name="mm_vt",
    )(*args)


def _deinterleave(w, width):
    lead = w.shape[:-1]
    n = w.shape[-1] // width
    w = w.reshape(lead + (n, width // 2, 2))
    w = jnp.swapaxes(w, -1, -2)
    return w.reshape(lead + (n * width,))


def _pad_groups(w, width, to):
    lead = w.shape[:-1]
    n = w.shape[-1] // width
    w = w.reshape(lead + (n, width))
    w = jnp.pad(w, [(0, 0)] * len(lead) + [(0, 0), (0, to - width)])
    return w.reshape(lead + (n * to,))


def _rope_angles(seq_len, rot_dim):
    rows = seq_len // GRID_W
    row, col = jnp.meshgrid(jnp.arange(rows), jnp.arange(GRID_W), indexing="ij")
    row = row.reshape(-1).astype(F32)
    col = col.reshape(-1).astype(F32)
    axis_dim = rot_dim // 2
    inv_freq = ROPE_THETA ** (-jnp.arange(0, axis_dim, 2, dtype=F32) / axis_dim)
    ang = jnp.concatenate([row[:, None] * inv_freq, col[:, None] * inv_freq], axis=-1)
    return jnp.cos(ang), jnp.sin(ang)


def _rope_tables(seq_len, rot_dim, width):
    cos, sin = _rope_angles(seq_len, rot_dim)
    half = rot_dim // 2
    z = jnp.zeros((seq_len, width - rot_dim), F32)
    zh = jnp.zeros_like(sin)
    c = jnp.concatenate([cos, cos, z], axis=-1)
    if rot_dim == width:
        return (half,), [c, jnp.concatenate([-sin, sin], axis=-1)]
    s_a = jnp.concatenate([-sin, zh, z], axis=-1)
    s_b = jnp.concatenate([zh, sin, z], axis=-1)
    return (width - half, half), [c, s_a, s_b]


def kernel(x, mem, ffn1_norm, ffn1_w_gu, ffn1_w_down, mix_norm, w_o, mem_norm, w_mem_kv, mem_q_norm, mem_k_norm, ffn2_norm, ffn2_w_gu, ffn2_w_down, a_w_in, a_q_a_norm, a_kv_a_norm, a_w_q_b, a_w_kv_b, a_q_nope_norm, a_q_pe_norm, a_k_nope_norm, a_k_pe_norm, b_w_in, b_q_norm, b_k_norm):
    b, s, d = x.shape
    n = b * s
    m_tok = mem.shape[1]
    depth = ffn1_norm.shape[0]
    hd = HEAD_DIM
    mem_w = w_mem_kv.shape[2] // 2
    mem_hd = mem_w // MEM_HEADS
    tok_w = w_o.shape[1] - mem_w
    n_heads = tok_w // hd
    q_lora = a_q_a_norm.shape[1]
    kv_lora = a_kv_a_norm.shape[1]
    n_kv = n_heads // B_GROUP

    tm = min(1024, s)
    seq_tiles = s // tm
    tq = min(1024, s)
    tk = min(512, s // 2)
    tm_mem = min(512, b * m_tok)
    tm_norm = min(256, s)

    def tile(width, pref):
        t = min(pref, width)
        while width % t:
            t -= hd
        return t

    rope_a = _rope_tables(s, A_ROPE, hd)
    rope_b = _rope_tables(s, hd, hd)
    scale_a = float((hd + A_ROPE) ** -0.5) * LOG2E
    scale_b = float(hd ** -0.5) * LOG2E
    scale_m = float(mem_hd ** -0.5)

    xf = x.reshape(n, d)
    memf = mem.reshape(b * m_tok, d)
    w_gu1, w_gu2 = ffn1_w_gu.astype(BF16), ffn2_w_gu.astype(BF16)
    w_dn1, w_dn2 = ffn1_w_down.astype(BF16), ffn2_w_down.astype(BF16)
    w_ob = w_o.astype(BF16)
    d_ff = w_gu1.shape[2] // 2
    r3 = lambda t: t.reshape(b, s, t.shape[-1])

    def produce(a_list, w, scale, g_next):
        x_new, xg, ss = mm_residual(a_list, w, i, xf, scale, g_next, tm, tile(d, 512))
        return x_new, xg, row_scale(ss, d, tm)

    xg, r = scale_prep(xf, ffn1_norm[0], tm_norm)
    for i in range(depth):
        act = mm_swiglu(xg, r, w_gu1, i, tm, tile(d_ff, 512))
        xf, xg, r = produce([act], w_dn1, 0.5, mix_norm[i])
        proj = functools.partial(mm_headnorm, r=r)
        j = i // N_MIXERS
        if i % N_MIXERS == 0:
            w_in = a_w_in[j]
            o1, o2, o3 = q_lora, q_lora + kv_lora, q_lora + kv_lora + A_ROPE
            w_cq = w_in[:, :o1].astype(BF16)
            w_ckv = w_in[:, o1:o2].astype(BF16)
            w_kpe = _pad_groups(_deinterleave(w_in[:, o2:o3], A_ROPE), A_ROPE, hd).astype(BF16)
            w_qm = w_in[:, o3:].astype(BF16)
            wq = a_w_q_b[j].reshape(q_lora, n_heads, hd + A_ROPE)
            w_qn = wq[:, :, :hd].reshape(q_lora, n_heads * hd).astype(BF16)
            w_qp = _pad_groups(_deinterleave(wq[:, :, hd:].reshape(q_lora, n_heads * A_ROPE), A_ROPE),
                               A_ROPE, hd).astype(BF16)
            wkv = a_w_kv_b[j].reshape(kv_lora, n_heads, 2 * hd)
            w_kn = wkv[:, :, :hd].reshape(kv_lora, n_heads * hd).astype(BF16)
            w_v = wkv[:, :, hd:].reshape(kv_lora, n_heads * hd).astype(BF16)
            g_qpe = _pad_groups(_deinterleave(a_q_pe_norm[j], A_ROPE), A_ROPE, hd) * scale_a
            g_kpe = _pad_groups(_deinterleave(a_k_pe_norm[j], A_ROPE), A_ROPE, hd)

            cqn = proj(xg, w_cq, a_q_a_norm[j], q_lora, q_lora, tm, q_lora)
            ckvn = proj(xg, w_ckv, a_kv_a_norm[j], kv_lora, kv_lora, tm, kv_lora)
            kpe = proj(xg, w_kpe, g_kpe, hd, A_ROPE, tm, hd, rope=rope_a, seq_tiles=seq_tiles)
            mq = proj(xg, w_qm, mem_q_norm[i] * scale_m, mem_hd, mem_hd, tm, tile(mem_w, 1024))
            tw = tile(n_heads * hd, 1024)
            qn = mm_headnorm(cqn, w_qn, a_q_nope_norm[j] * scale_a, hd, hd, tm, tw)
            qp = mm_headnorm(cqn, w_qp, g_qpe, hd, A_ROPE, tm, tw, rope=rope_a, seq_tiles=seq_tiles)
            kn = mm_headnorm(ckvn, w_kn, a_k_nope_norm[j], hd, hd, tm, tw)
            vt = mm_vt(ckvn, w_v.T, tk, tw).reshape(b, s // tk, n_heads * hd, tk)
            tok = attention_t([r3(qn), r3(qp)], [r3(kn), r3(kpe)], vt, n_heads, 1, tq)
        else:
            w_in = b_w_in[j]
            qw, kw = n_heads * hd, n_kv * hd
            w_q = _deinterleave(w_in[:, :qw], hd).astype(BF16)
            w_k = _deinterleave(w_in[:, qw:qw + kw], hd).astype(BF16)
            w_v = w_in[:, qw + kw:qw + 2 * kw].astype(BF16)
            w_qm = w_in[:, qw + 2 * kw:].astype(BF16)
            g_q = _deinterleave(b_q_norm[j], hd) * scale_b
            g_k = _deinterleave(b_k_norm[j], hd)
            q = proj(xg, w_q, g_q, hd, hd, tm, tile(qw, 1024), rope=rope_b, seq_tiles=seq_tiles)
            k = proj(xg, w_k, g_k, hd, hd, tm, tile(kw, 1024), rope=rope_b, seq_tiles=seq_tiles)
            vt = mm_vt(xg, w_v.T, tk, tile(kw, 1024), r=r).reshape(b, s // tk, kw, tk)
            mq = proj(xg, w_qm, mem_q_norm[i] * scale_m, mem_hd, mem_hd, tm, tile(mem_w, 1024))
            tok = attention_t([r3(q)], [r3(k)], vt, n_heads, B_GROUP, tq)

        memn = rmsnorm(memf, mem_norm[i], tm_mem)
        w_mkv = w_mem_kv[i].astype(BF16)
        mk = mm_headnorm(memn, w_mkv[:, :mem_w], mem_k_norm[i], mem_hd, mem_hd, tm_mem, tile(mem_w, 1024))
        mv = mm_cast(memn, w_mkv[:, mem_w:], tm_mem, tile(mem_w, 1024))
        mo = attention([mq.reshape(b, s, mem_w)], [mk.reshape(b, m_tok, mem_w)], mv.reshape(b, m_tok, mem_w),
                       MEM_HEADS, 1, mem_hd, mem_hd, tq, m_tok)
        xf, xg, r = produce([tok.reshape(n, tok_w), mo.reshape(n, mem_w)], w_ob, 1.0, ffn2_norm[i])
        act = mm_swiglu(xg, r, w_gu2, i, tm, tile(d_ff, 512))
        if i + 1 < depth:
            xf, xg, r = produce([act], w_dn2, 0.5, ffn1_norm[i + 1])
        else:
            xf = mm_residual([act], w_dn2, i, xf, 0.5, None, tm, tile(d, 512))
    return xf.reshape(b, s, d)
```

```python
import functools

import jax
import jax.numpy as jnp
from jax import lax
from jax.experimental import pallas as pl
from jax.experimental.pallas import tpu as pltpu

EPS = 1e-6
GRID_W = 64
ROPE_THETA = 10000.0
LANES = 128
HEAD_DIM = LANES
A_ROPE = 64
MEM_HEADS = 4
LOG2E = 1.4426950408889634
N_MIXERS = 2
B_GROUP = 4
ONES_ROWS = 16
V7X_VMEM_BYTES = 64 * 1024 * 1024
VMEM_LIMIT = V7X_VMEM_BYTES * 7 // 8

F32 = jnp.float32
BF16 = jnp.bfloat16


def _params():
    return pltpu.CompilerParams(vmem_limit_bytes=VMEM_LIMIT)


def _rmsnorm_kernel(x_ref, g_ref, o_ref):
    x = x_ref[...]
    ms = jnp.mean(x * x, axis=-1, keepdims=True)
    o_ref[...] = (x * lax.rsqrt(ms + EPS) * g_ref[...]).astype(o_ref.dtype)


def rmsnorm(x, g, tm):
    m, d = x.shape
    return pl.pallas_call(
        _rmsnorm_kernel,
        grid=(m // tm,),
        in_specs=[pl.BlockSpec((tm, d), lambda i: (i, 0)),
                  pl.BlockSpec((1, d), lambda i: (0, 0))],
        out_specs=pl.BlockSpec((tm, d), lambda i: (i, 0)),
        out_shape=jax.ShapeDtypeStruct((m, d), BF16),
        compiler_params=_params(),
        name="rmsnorm",
    )(x, g.reshape(1, d))


def _row_scale_kernel(ss_ref, r_ref, *, inv_d):
    ss = ss_ref[...]
    tot = ss[:, :LANES]
    for c in range(1, ss.shape[1] // LANES):
        tot = tot + ss[:, c * LANES:(c + 1) * LANES]
    r_ref[...] = lax.rsqrt(tot * inv_d + EPS)


def row_scale(ss, d_norm, tm):
    m, w = ss.shape
    return pl.pallas_call(
        functools.partial(_row_scale_kernel, inv_d=1.0 / d_norm),
        grid=(m // tm,),
        in_specs=[pl.BlockSpec((tm, w), lambda i: (i, 0))],
        out_specs=pl.BlockSpec((tm, LANES), lambda i: (i, 0)),
        out_shape=jax.ShapeDtypeStruct((m, LANES), F32),
        compiler_params=_params(),
        name="row_scale",
    )(ss)


def _widen(r, width):
    return r if width == LANES else jnp.tile(r, (1, width // LANES))


def _scale_prep_kernel(x_ref, g_ref, xg_ref, r_ref):
    x = x_ref[...]
    xg_ref[...] = (x * g_ref[...]).astype(BF16)
    r = lax.rsqrt(jnp.mean(x * x, axis=-1, keepdims=True) + EPS)
    r_ref[...] = jnp.broadcast_to(r, r_ref.shape)


def scale_prep(x, g, tm):
    m, d = x.shape
    return pl.pallas_call(
        _scale_prep_kernel,
        grid=(m // tm,),
        in_specs=[pl.BlockSpec((tm, d), lambda i: (i, 0)),
                  pl.BlockSpec((1, d), lambda i: (0, 0))],
        out_specs=[pl.BlockSpec((tm, d), lambda i: (i, 0)),
                   pl.BlockSpec((tm, LANES), lambda i: (i, 0))],
        out_shape=[jax.ShapeDtypeStruct((m, d), BF16), jax.ShapeDtypeStruct((m, LANES), F32)],
        compiler_params=_params(),
        name="scale_prep",
    )(x, g.reshape(1, d))


def _mm_cast_kernel(a_ref, b_ref, o_ref):
    acc = jnp.dot(a_ref[...], b_ref[...], preferred_element_type=F32)
    o_ref[...] = acc.astype(o_ref.dtype)


def _mm_swiglu_kernel(a_ref, r_ref, bg_ref, bu_ref, o_ref):
    a = a_ref[...]
    r = _widen(r_ref[...], o_ref.shape[1])
    g = jnp.dot(a, bg_ref[...], preferred_element_type=F32) * r
    u = jnp.dot(a, bu_ref[...], preferred_element_type=F32) * r
    o_ref[...] = (g * (1.0 / (1.0 + jnp.exp(-g))) * u).astype(o_ref.dtype)


def _mm_residual_kernel(*refs, n_a, scale, emit_next):
    a_refs = refs[:n_a]
    b_refs = refs[n_a:2 * n_a]
    r_ref = refs[2 * n_a]
    acc = jnp.dot(a_refs[0][...], b_refs[0][...], preferred_element_type=F32)
    for a_ref, b_ref in zip(a_refs[1:], b_refs[1:]):
        acc = acc + jnp.dot(a_ref[...], b_ref[...], preferred_element_type=F32)
    x = r_ref[...] + (acc if scale == 1.0 else scale * acc)
    if emit_next:
        g_ref = refs[2 * n_a + 1]
        o_ref, xg_ref, ss_ref = refs[-3:]
        xg_ref[...] = (x * g_ref[...]).astype(BF16)
        ss_ref[...] = jnp.broadcast_to(jnp.sum(x * x, axis=-1, keepdims=True), ss_ref.shape)
    else:
        o_ref = refs[-1]
    o_ref[...] = x


def _mm_headnorm_kernel(*refs, cw, inv_count, shifts, scaled):
    a_ref, b_ref = refs[0], refs[1]
    nxt = 2
    acc = jnp.dot(a_ref[...], b_ref[...], preferred_element_type=F32)
    if scaled:
        acc = acc * _widen(refs[2][...], acc.shape[1])
        nxt = 3
    g = refs[nxt][...]
    tabs = refs[nxt + 1:-1]
    o_ref = refs[-1]
    for c in range(acc.shape[1] // cw):
        x = acc[:, c * cw:(c + 1) * cw]
        ms = jnp.sum(x * x, axis=-1, keepdims=True) * inv_count
        y = x * lax.rsqrt(ms + EPS) * g
        if shifts:
            out = y * tabs[0][...]
            for k, sh in enumerate(shifts):
                out = out + pltpu.roll(y, sh, 1) * tabs[1 + k][...]
            y = out
        o_ref[:, c * cw:(c + 1) * cw] = y.astype(o_ref.dtype)


def _r_spec(tm):
    return pl.BlockSpec((tm, LANES), lambda i, j: (i, 0))


def mm_cast(a, b, tm, tn, out_dtype=BF16):
    m, k = a.shape
    n_out = b.shape[1]
    return pl.pallas_call(
        _mm_cast_kernel,
        grid=(m // tm, n_out // tn),
        in_specs=[pl.BlockSpec((tm, k), lambda i, j: (i, 0)),
                  pl.BlockSpec((k, tn), lambda i, j: (0, j))],
        out_specs=pl.BlockSpec((tm, tn), lambda i, j: (i, j)),
        out_shape=jax.ShapeDtypeStruct((m, n_out), out_dtype),
        compiler_params=_params(),
        name="mm_cast",
    )(a, b)


def mm_swiglu(xg, r, w_gu, layer, tm, tn):
    m, k = xg.shape
    d_ff = w_gu.shape[2] // 2
    off = d_ff // tn
    return pl.pallas_call(
        _mm_swiglu_kernel,
        grid=(m // tm, d_ff // tn),
        in_specs=[pl.BlockSpec((tm, k), lambda i, j: (i, 0)),
                  _r_spec(tm),
                  pl.BlockSpec((None, k, tn), lambda i, j: (layer, 0, j)),
                  pl.BlockSpec((None, k, tn), lambda i, j: (layer, 0, j + off))],
        out_specs=pl.BlockSpec((tm, tn), lambda i, j: (i, j)),
        out_shape=jax.ShapeDtypeStruct((m, d_ff), BF16),
        compiler_params=_params(),
        name="mm_swiglu",
    )(xg, r, w_gu, w_gu)


def mm_residual(a_list, w, layer, res, scale, g_next, tm, tn):
    m = a_list[0].shape[0]
    n_out = w.shape[2]
    in_specs = [pl.BlockSpec((tm, a.shape[1]), lambda i, j: (i, 0)) for a in a_list]
    row = 0
    for a in a_list:
        ka = a.shape[1]
        assert row % ka == 0
        in_specs.append(pl.BlockSpec((None, ka, tn), functools.partial(lambda i, j, rb: (layer, rb, j), rb=row // ka)))
        row += ka
    assert row == w.shape[1]
    tile_spec = pl.BlockSpec((tm, tn), lambda i, j: (i, j))
    in_specs.append(tile_spec)
    args = list(a_list) + [w] * len(a_list) + [res]
    emit_next = g_next is not None
    if emit_next:
        in_specs.append(pl.BlockSpec((1, tn), lambda i, j: (0, j)))
        args.append(g_next.reshape(1, n_out))
        out_specs = [tile_spec, tile_spec, pl.BlockSpec((tm, LANES), lambda i, j: (i, j))]
        out_shape = [jax.ShapeDtypeStruct((m, n_out), F32), jax.ShapeDtypeStruct((m, n_out), BF16),
                     jax.ShapeDtypeStruct((m, (n_out // tn) * LANES), F32)]
    else:
        out_specs = tile_spec
        out_shape = jax.ShapeDtypeStruct((m, n_out), F32)
    return pl.pallas_call(
        functools.partial(_mm_residual_kernel, n_a=len(a_list), scale=scale, emit_next=emit_next),
        grid=(m // tm, n_out // tn),
        in_specs=in_specs,
        out_specs=out_specs,
        out_shape=out_shape,
        compiler_params=_params(),
        name="mm_residual",
    )(*args)


def mm_headnorm(a, b, g, cw, count, tm, tn, rope=None, seq_tiles=1, r=None):
    m, k = a.shape
    n_out = b.shape[1]
    in_specs = [pl.BlockSpec((tm, k), lambda i, j: (i, 0)),
                pl.BlockSpec((k, tn), lambda i, j: (0, j))]
    args = [a, b]
    if r is not None:
        in_specs.append(_r_spec(tm))
        args.append(r)
    in_specs.append(pl.BlockSpec((1, cw), lambda i, j: (0, 0)))
    args.append(g.reshape(1, cw))
    shifts = ()
    if rope is not None:
        shifts, tabs = rope
        for t in tabs:
            in_specs.append(pl.BlockSpec((tm, cw), lambda i, j: (i % seq_tiles, 0)))
            args.append(t)
    kern = functools.partial(_mm_headnorm_kernel, cw=cw, inv_count=1.0 / count, shifts=tuple(shifts),
                             scaled=r is not None)
    return pl.pallas_call(
        kern,
        grid=(m // tm, n_out // tn),
        in_specs=in_specs,
        out_specs=pl.BlockSpec((tm, tn), lambda i, j: (i, j)),
        out_shape=jax.ShapeDtypeStruct((m, n_out), BF16),
        compiler_params=_params(),
        name="mm_headnorm",
    )(*args)


def _attn_kernel(*refs, nparts, tk, nkv):
    q_refs = refs[:nparts]
    k_refs = refs[nparts:2 * nparts]
    v_ref = refs[2 * nparts]
    o_ref = refs[2 * nparts + 1]
    if nparts > 1:
        q = jnp.concatenate([r[...] for r in q_refs], axis=-1)
    else:
        q = q_refs[0][...]
    tq = q.shape[0]
    dv = v_ref.shape[-1]

    def body(t, carry):
        m, l, acc = carry
        off = pl.multiple_of(t * tk, tk)
        if nparts > 1:
            k = jnp.concatenate([r[pl.ds(off, tk), :] for r in k_refs], axis=-1)
        else:
            k = k_refs[0][pl.ds(off, tk), :]
        s = lax.dot_general(q, k, (((1,), (1,)), ((), ())), preferred_element_type=F32)
        m_new = jnp.maximum(m, jnp.max(s, axis=-1, keepdims=True))
        p = jnp.exp(s - m_new)
        alpha = jnp.exp(m - m_new)
        l = alpha * l + jnp.sum(p, axis=-1, keepdims=True)
        acc = alpha * acc + jnp.dot(p.astype(BF16), v_ref[pl.ds(off, tk), :], preferred_element_type=F32)
        return m_new, l, acc

    init = (jnp.full((tq, 1), -jnp.inf, F32), jnp.zeros((tq, 1), F32), jnp.zeros((tq, dv), F32))
    if nkv == 1:
        m, l, acc = body(0, init)
    else:
        m, l, acc = lax.fori_loop(0, nkv, body, init)
    o_ref[...] = (acc / l).astype(o_ref.dtype)


def attention(qs, ks, v, n_heads, group, dq, dv, tq, tk):
    b, s, _ = qs[0].shape
    skv = v.shape[1]
    nparts = len(qs)
    in_specs = [pl.BlockSpec((None, tq, dq), lambda bi, h, qi: (bi, qi, h)) for _ in qs]
    for k in ks:
        if k.shape[2] == dq:
            in_specs.append(pl.BlockSpec((None, skv, dq), lambda bi, h, qi: (bi, 0, 0)))
        else:
            in_specs.append(pl.BlockSpec((None, skv, dq), lambda bi, h, qi: (bi, 0, h // group)))
    in_specs.append(pl.BlockSpec((None, skv, dv), lambda bi, h, qi: (bi, 0, h // group)))
    kern = functools.partial(_attn_kernel, nparts=nparts, tk=tk, nkv=skv // tk)
    return pl.pallas_call(
        kern,
        grid=(b, n_heads, s // tq),
        in_specs=in_specs,
        out_specs=pl.BlockSpec((None, tq, dv), lambda bi, h, qi: (bi, qi, h)),
        out_shape=jax.ShapeDtypeStruct((b, s, n_heads * dv), BF16),
        compiler_params=_params(),
        name="attention",
    )(*qs, *ks, v)


def _attn_t_kernel(*refs, nparts, tk, nkv):
    q_refs = refs[:nparts]
    k_refs = refs[nparts:2 * nparts]
    vt_ref = refs[2 * nparts]
    o_ref = refs[2 * nparts + 1]
    scratch = refs[2 * nparts + 2:]
    dv = vt_ref.shape[1]
    first_q_tile = pl.program_id(2) == 0
    if nparts > 1:
        kcat_ref, vx_ref, s_ref, m_ref, acc_ref = scratch
        q = jnp.concatenate([r[...] for r in q_refs], axis=-1)

        @pl.when(first_q_tile)
        def _():
            for i, r in enumerate(k_refs):
                kcat_ref[:, i * HEAD_DIM:(i + 1) * HEAD_DIM] = r[...]
        k_ref = kcat_ref
    else:
        vx_ref, s_ref, m_ref, acc_ref = scratch
        q = q_refs[0][...]
        k_ref = k_refs[0]

    @pl.when(first_q_tile)
    def _():
        vx_ref[:, :dv, :] = vt_ref[...]
        vx_ref[:, dv:, :] = jnp.ones((nkv, ONES_ROWS, tk), BF16)

    qt = q.astype(F32).T.astype(BF16)

    def qk(c, slot):
        off = pl.multiple_of(c * tk, tk)
        s_ref[slot] = jnp.dot(k_ref[pl.ds(off, tk), :], qt, preferred_element_type=F32)

    def softmax_pv(c, slot):
        s = s_ref[slot]
        m_old = m_ref[...]
        m_new = jnp.maximum(m_old, jnp.max(s, axis=0, keepdims=True))
        alpha = jnp.exp2(m_old - m_new)
        p = jnp.exp2(s - m_new)
        m_ref[...] = m_new
        acc_ref[...] = alpha * acc_ref[...] + jnp.dot(vx_ref[c], p.astype(BF16), preferred_element_type=F32)

    m_ref[...] = jnp.full(m_ref.shape, -jnp.inf, F32)
    acc_ref[...] = jnp.zeros(acc_ref.shape, F32)
    qk(0, 0)

    def pair(i, carry):
        c = 2 * i
        qk(c + 1, 1)
        softmax_pv(c, 0)
        qk(c + 2, 0)
        softmax_pv(c + 1, 1)
        return carry

    lax.fori_loop(0, nkv // 2 - 1, pair, 0)
    qk(nkv - 1, 1)
    softmax_pv(nkv - 2, 0)
    softmax_pv(nkv - 1, 1)
    o_ref[...] = (acc_ref[:dv, :] * (1.0 / acc_ref[dv:dv + 1, :])).T.astype(o_ref.dtype)


def attention_t(qs, ks, vt, n_heads, group, tq):
    b, s, _ = qs[0].shape
    skv = ks[0].shape[1]
    nkv, tk = vt.shape[1], vt.shape[3]
    n_kv = n_heads // group
    dv = vt.shape[2] // n_kv
    nparts = len(qs)
    assert nkv % 2 == 0 and nkv * tk == skv
    in_specs = [pl.BlockSpec((None, tq, HEAD_DIM), lambda bi, h, qi: (bi, qi, h)) for _ in qs]
    for k in ks:
        if k.shape[2] == n_kv * HEAD_DIM:
            in_specs.append(pl.BlockSpec((None, skv, HEAD_DIM), lambda bi, h, qi: (bi, 0, h // group)))
        else:
            in_specs.append(pl.BlockSpec((None, skv, HEAD_DIM), lambda bi, h, qi: (bi, 0, 0)))
    in_specs.append(pl.BlockSpec((None, nkv, dv, tk), lambda bi, h, qi: (bi, 0, h // group, 0)))
    scratch = []
    if nparts > 1:
        scratch.append(pltpu.VMEM((skv, nparts * HEAD_DIM), BF16))
    scratch += [pltpu.VMEM((nkv, dv + ONES_ROWS, tk), BF16), pltpu.VMEM((2, tk, tq), F32),
                pltpu.VMEM((1, tq), F32), pltpu.VMEM((dv + ONES_ROWS, tq), F32)]
    kern = functools.partial(_attn_t_kernel, nparts=nparts, tk=tk, nkv=nkv)
    return pl.pallas_call(
        kern,
        grid=(b, n_heads, s // tq),
        in_specs=in_specs,
        out_specs=pl.BlockSpec((None, tq, dv), lambda bi, h, qi: (bi, qi, h)),
        out_shape=jax.ShapeDtypeStruct((b, s, n_heads * dv), BF16),
        scratch_shapes=scratch,
        compiler_params=_params(),
        name="attention_t",
    )(*qs, *ks, vt)


def _mm_vt_kernel(*refs, scaled):
    a_ref, wt_ref, o_ref = refs[0], refs[1], refs[-1]
    out = lax.dot_general(wt_ref[...], a_ref[...], (((1,), (1,)), ((), ())), preferred_element_type=F32)
    if scaled:
        out = out * refs[2][...].T[:1, :]
    o_ref[...] = out.astype(o_ref.dtype)


def mm_vt(a, wt, tk, tn, r=None):
    m, k = a.shape
    cols = wt.shape[0]
    in_specs = [pl.BlockSpec((tk, k), lambda i, j: (i, 0)),
                pl.BlockSpec((tn, k), lambda i, j: (j, 0))]
    args = [a, wt]
    if r is not None:
        in_specs.append(_r_spec(tk))
        args.append(r)
    return pl.pallas_call(
        functools.partial(_mm_vt_kernel, scaled=r is not None),
        grid=(m // tk, cols // tn),
        in_specs=in_specs,
        out_specs=pl.BlockSpec((None, tn, tk), lambda i, j: (i, j, 0)),
        out_shape=jax.ShapeDtypeStruct((m // tk, cols, tk), BF16),
        compiler_params=_params(),
        name="mm_vt",
    )(*args)


def _deinterleave(w, width):
    lead = w.shape[:-1]
    n = w.shape[-1] // width
    w = w.reshape(lead + (n, width // 2, 2))
    w = jnp.swapaxes(w, -1, -2)
    return w.reshape(lead + (n * width,))


def _pad_groups(w, width, to):
    lead = w.shape[:-1]
    n = w.shape[-1] // width
    w = w.reshape(lead + (n, width))
    w = jnp.pad(w, [(0, 0)] * len(lead) + [(0, 0), (0, to - width)])
    return w.reshape(lead + (n * to,))


def _rope_angles(seq_len, rot_dim):
    rows = seq_len // GRID_W
    row, col = jnp.meshgrid(jnp.arange(rows), jnp.arange(GRID_W), indexing="ij")
    row = row.reshape(-1).astype(F32)
    col = col.reshape(-1).astype(F32)
    axis_dim = rot_dim // 2
    inv_freq = ROPE_THETA ** (-jnp.arange(0, axis_dim, 2, dtype=F32) / axis_dim)
    ang = jnp.concatenate([row[:, None] * inv_freq, col[:, None] * inv_freq], axis=-1)
    return jnp.cos(ang), jnp.sin(ang)


def _rope_tables(seq_len, rot_dim, width):
    cos, sin = _rope_angles(seq_len, rot_dim)
    half = rot_dim // 2
    z = jnp.zeros((seq_len, width - rot_dim), F32)
    zh = jnp.zeros_like(sin)
    c = jnp.concatenate([cos, cos, z], axis=-1)
    if rot_dim == width:
        return (half,), [c, jnp.concatenate([-sin, sin], axis=-1)]
    s_a = jnp.concatenate([-sin, zh, z], axis=-1)
    s_b = jnp.concatenate([zh, sin, z], axis=-1)
    return (width - half, half), [c, s_a, s_b]


def kernel(x, mem, ffn1_norm, ffn1_w_gu, ffn1_w_down, mix_norm, w_o, mem_norm, w_mem_kv, mem_q_norm, mem_k_norm, ffn2_norm, ffn2_w_gu, ffn2_w_down, a_w_in, a_q_a_norm, a_kv_a_norm, a_w_q_b, a_w_kv_b, a_q_nope_norm, a_q_pe_norm, a_k_nope_norm, a_k_pe_norm, b_w_in, b_q_norm, b_k_norm):
    b, s, d = x.shape
    n = b * s
    m_tok = mem.shape[1]
    depth = ffn1_norm.shape[0]
    hd = HEAD_DIM
    mem_w = w_mem_kv.shape[2] // 2
    mem_hd = mem_w // MEM_HEADS
    tok_w = w_o.shape[1] - mem_w
    n_heads = tok_w // hd
    q_lora = a_q_a_norm.shape[1]
    kv_lora = a_kv_a_norm.shape[1]
    n_kv = n_heads // B_GROUP

    tm = min(1024, s)
    seq_tiles = s // tm
    tq = min(1024, s)
    tk = min(512, s // 2)
    tm_mem = min(512, b * m_tok)
    tm_norm = min(256, s)

    def tile(width, pref):
        t = min(pref, width)
        while width % t:
            t -= hd
        return t

    rope_a = _rope_tables(s, A_ROPE, hd)
    rope_b = _rope_tables(s, hd, hd)
    scale_a = float((hd + A_ROPE) ** -0.5) * LOG2E
    scale_b = float(hd ** -0.5) * LOG2E
    scale_m = float(mem_hd ** -0.5)

    xf = x.reshape(n, d)
    memf = mem.reshape(b * m_tok, d)
    w_gu1, w_gu2 = ffn1_w_gu.astype(BF16), ffn2_w_gu.astype(BF16)
    w_dn1, w_dn2 = ffn1_w_down.astype(BF16), ffn2_w_down.astype(BF16)
    w_ob = w_o.astype(BF16)
    d_ff = w_gu1.shape[2] // 2
    r3 = lambda t: t.reshape(b, s, t.shape[-1])

    def produce(a_list, w, scale, g_next):
        x_new, xg, ss = mm_residual(a_list, w, i, xf, scale, g_next, tm, tile(d, 512))
        return x_new, xg, row_scale(ss, d, tm)

    xg, r = scale_prep(xf, ffn1_norm[0], tm_norm)
    for i in range(depth):
        act = mm_swiglu(xg, r, w_gu1, i, tm, tile(d_ff, 512))
        xf, xg, r = produce([act], w_dn1, 0.5, mix_norm[i])
        proj = functools.partial(mm_headnorm, r=r)
        j = i // N_MIXERS
        if i % N_MIXERS == 0:
            w_in = a_w_in[j]
            o1, o2, o3 = q_lora, q_lora + kv_lora, q_lora + kv_lora + A_ROPE
            w_cq = w_in[:, :o1].astype(BF16)
            w_ckv = w_in[:, o1:o2].astype(BF16)
            w_kpe = _pad_groups(_deinterleave(w_in[:, o2:o3], A_ROPE), A_ROPE, hd).astype(BF16)
            w_qm = w_in[:, o3:].astype(BF16)
            wq = a_w_q_b[j].reshape(q_lora, n_heads, hd + A_ROPE)
            w_qn = wq[:, :, :hd].reshape(q_lora, n_heads * hd).astype(BF16)
            w_qp = _pad_groups(_deinterleave(wq[:, :, hd:].reshape(q_lora, n_heads * A_ROPE), A_ROPE),
                               A_ROPE, hd).astype(BF16)
            wkv = a_w_kv_b[j].reshape(kv_lora, n_heads, 2 * hd)
            w_kn = wkv[:, :, :hd].reshape(kv_lora, n_heads * hd).astype(BF16)
            w_v = wkv[:, :, hd:].reshape(kv_lora, n_heads * hd).astype(BF16)
            g_qpe = _pad_groups(_deinterleave(a_q_pe_norm[j], A_ROPE), A_ROPE, hd) * scale_a
            g_kpe = _pad_groups(_deinterleave(a_k_pe_norm[j], A_ROPE), A_ROPE, hd)

            cqn = proj(xg, w_cq, a_q_a_norm[j], q_lora, q_lora, tm, q_lora)
            ckvn = proj(xg, w_ckv, a_kv_a_norm[j], kv_lora, kv_lora, tm, kv_lora)
            kpe = proj(xg, w_kpe, g_kpe, hd, A_ROPE, tm, hd, rope=rope_a, seq_tiles=seq_tiles)
            mq = proj(xg, w_qm, mem_q_norm[i] * scale_m, mem_hd, mem_hd, tm, tile(mem_w, 1024))
            tw = tile(n_heads * hd, 1024)
            qn = mm_headnorm(cqn, w_qn, a_q_nope_norm[j] * scale_a, hd, hd, tm, tw)
            qp = mm_headnorm(cqn, w_qp, g_qpe, hd, A_ROPE, tm, tw, rope=rope_a, seq_tiles=seq_tiles)
            kn = mm_headnorm(ckvn, w_kn, a_k_nope_norm[j], hd, hd, tm, tw)
            vt = mm_vt(ckvn, w_v.T, tk, tw).reshape(b, s // tk, n_heads * hd, tk)
            tok = attention_t([r3(qn), r3(qp)], [r3(kn), r3(kpe)], vt, n_heads, 1, tq)
        else:
            w_in = b_w_in[j]
            qw, kw = n_heads * hd, n_kv * hd
            w_q = _deinterleave(w_in[:, :qw], hd).astype(BF16)
            w_k = _deinterleave(w_in[:, qw:qw + kw], hd).astype(BF16)
            w_v = w_in[:, qw + kw:qw + 2 * kw].astype(BF16)
            w_qm = w_in[:, qw + 2 * kw:].astype(BF16)
            g_q = _deinterleave(b_q_norm[j], hd) * scale_b
            g_k = _deinterleave(b_k_norm[j], hd)
            q = proj(xg, w_q, g_q, hd, hd, tm, tile(qw, 1024), rope=rope_b, seq_tiles=seq_tiles)
            k = proj(xg, w_k, g_k, hd, hd, tm, tile(kw, 1024), rope=rope_b, seq_tiles=seq_tiles)
            vt = mm_vt(xg, w_v.T, tk, tile(kw, 1024), r=r).reshape(b, s // tk, kw, tk)
            mq = proj(xg, w_qm, mem_q_norm[i] * scale_m, mem_hd, mem_hd, tm, tile(mem_w, 1024))
            tok = attention_t([r3(q)], [r3(k)], vt, n_heads, B_GROUP, tq)

        memn = rmsnorm(memf, mem_norm[i], tm_mem)
        w_mkv = w_mem_kv[i].astype(BF16)
        mk = mm_headnorm(memn, w_mkv[:, :mem_w], mem_k_norm[i], mem_hd, mem_hd, tm_mem, tile(mem_w, 1024))
        mv = mm_cast(memn, w_mkv[:, mem_w:], tm_mem, tile(mem_w, 1024))
        mo = attention([mq.reshape(b, s, mem_w)], [mk.reshape(b, m_tok, mem_w)], mv.reshape(b, m_tok, mem_w),
                       MEM_HEADS, 1, mem_hd, mem_hd, tq, m_tok)
        xf, xg, r = produce([tok.reshape(n, tok_w), mo.reshape(n, mem_w)], w_ob, 1.0, ffn2_norm[i])
        act = mm_swiglu(xg, r, w_gu2, i, tm, tile(d_ff, 512))
        if i + 1 < depth:
            xf, xg, r = produce([act], w_dn2, 0.5, ffn1_norm[i + 1])
        else:
            xf = mm_residual([act], w_dn2, i, xf, 0.5, None, tm, tile(d, 512))
    return xf.reshape(b, s, d)
```

```python
import functools

import jax
import jax.numpy as jnp
from jax import lax
from jax.experimental import pallas as pl
from jax.experimental.pallas import tpu as pltpu

EPS = 1e-6
GRID_W = 64
ROPE_THETA = 10000.0
LANES = 128
HEAD_DIM = LANES
A_ROPE = 64
MEM_HEADS = 4
LOG2E = 1.4426950408889634
N_MIXERS = 2
B_GROUP = 4
ONES_ROWS = 16
V7X_VMEM_BYTES = 64 * 1024 * 1024
VMEM_LIMIT = V7X_VMEM_BYTES * 7 // 8

F32 = jnp.float32
BF16 = jnp.bfloat16


def _params():
    return pltpu.CompilerParams(vmem_limit_bytes=VMEM_LIMIT)


def _rmsnorm_kernel(x_ref, g_ref, o_ref):
    x = x_ref[...]
    ms = jnp.mean(x * x, axis=-1, keepdims=True)
    o_ref[...] = (x * lax.rsqrt(ms + EPS) * g_ref[...]).astype(o_ref.dtype)


def rmsnorm(x, g, tm):
    m, d = x.shape
    return pl.pallas_call(
        _rmsnorm_kernel,
        grid=(m // tm,),
        in_specs=[pl.BlockSpec((tm, d), lambda i: (i, 0)),
                  pl.BlockSpec((1, d), lambda i: (0, 0))],
        out_specs=pl.BlockSpec((tm, d), lambda i: (i, 0)),
        out_shape=jax.ShapeDtypeStruct((m, d), BF16),
        compiler_params=_params(),
        name="rmsnorm",
    )(x, g.reshape(1, d))


def _row_scale_kernel(ss_ref, r_ref, *, inv_d):
    ss = ss_ref[...]
    tot = ss[:, :LANES]
    for c in range(1, ss.shape[1] // LANES):
        tot = tot + ss[:, c * LANES:(c + 1) * LANES]
    r_ref[...] = lax.rsqrt(tot * inv_d + EPS)


def row_scale(ss, d_norm, tm):
    m, w = ss.shape
    return pl.pallas_call(
        functools.partial(_row_scale_kernel, inv_d=1.0 / d_norm),
        grid=(m // tm,),
        in_specs=[pl.BlockSpec((tm, w), lambda i: (i, 0))],
        out_specs=pl.BlockSpec((tm, LANES), lambda i: (i, 0)),
        out_shape=jax.ShapeDtypeStruct((m, LANES), F32),
        compiler_params=_params(),
        name="row_scale",
    )(ss)


def _widen(r, width):
    return r if width == LANES else jnp.tile(r, (1, width // LANES))


def _scale_prep_kernel(x_ref, g_ref, xg_ref, r_ref):
    x = x_ref[...]
    xg_ref[...] = (x * g_ref[...]).astype(BF16)
    r = lax.rsqrt(jnp.mean(x * x, axis=-1, keepdims=True) + EPS)
    r_ref[...] = jnp.broadcast_to(r, r_ref.shape)


def scale_prep(x, g, tm):
    m, d = x.shape
    return pl.pallas_call(
        _scale_prep_kernel,
        grid=(m // tm,),
        in_specs=[pl.BlockSpec((tm, d), lambda i: (i, 0)),
                  pl.BlockSpec((1, d), lambda i: (0, 0))],
        out_specs=[pl.BlockSpec((tm, d), lambda i: (i, 0)),
                   pl.BlockSpec((tm, LANES), lambda i: (i, 0))],
        out_shape=[jax.ShapeDtypeStruct((m, d), BF16), jax.ShapeDtypeStruct((m, LANES), F32)],
        compiler_params=_params(),
        name="scale_prep",
    )(x, g.reshape(1, d))


def _mm_cast_kernel(a_ref, b_ref, o_ref):
    acc = jnp.dot(a_ref[...], b_ref[...], preferred_element_type=F32)
    o_ref[...] = acc.astype(o_ref.dtype)


def _mm_swiglu_kernel(a_ref, r_ref, bg_ref, bu_ref, o_ref):
    a = a_ref[...]
    r = _widen(r_ref[...], o_ref.shape[1])
    g = jnp.dot(a, bg_ref[...], preferred_element_type=F32) * r
    u = jnp.dot(a, bu_ref[...], preferred_element_type=F32) * r
    o_ref[...] = (g * (1.0 / (1.0 + jnp.exp(-g))) * u).astype(o_ref.dtype)


def _mm_residual_kernel(*refs, n_a, scale, emit_next):
    a_refs = refs[:n_a]
    b_refs = refs[n_a:2 * n_a]
    r_ref = refs[2 * n_a]
    acc = jnp.dot(a_refs[0][...], b_refs[0][...], preferred_element_type=F32)
    for a_ref, b_ref in zip(a_refs[1:], b_refs[1:]):
        acc = acc + jnp.dot(a_ref[...], b_ref[...], preferred_element_type=F32)
    x = r_ref[...] + (acc if scale == 1.0 else scale * acc)
    if emit_next:
        g_ref = refs[2 * n_a + 1]
        o_ref, xg_ref, ss_ref = refs[-3:]
        xg_ref[...] = (x * g_ref[...]).astype(BF16)
        ss_ref[...] = jnp.broadcast_to(jnp.sum(x * x, axis=-1, keepdims=True), ss_ref.shape)
    else:
        o_ref = refs[-1]
    o_ref[...] = x


def _mm_headnorm_kernel(*refs, cw, inv_count, shifts, scaled):
    a_ref, b_ref = refs[0], refs[1]
    nxt = 2
    acc = jnp.dot(a_ref[...], b_ref[...], preferred_element_type=F32)
    if scaled:
        acc = acc * _widen(refs[2][...], acc.shape[1])
        nxt = 3
    g = refs[nxt][...]
    tabs = refs[nxt + 1:-1]
    o_ref = refs[-1]
    for c in range(acc.shape[1] // cw):
        x = acc[:, c * cw:(c + 1) * cw]
        ms = jnp.sum(x * x, axis=-1, keepdims=True) * inv_count
        y = x * lax.rsqrt(ms + EPS) * g
        if shifts:
            out = y * tabs[0][...]
            for k, sh in enumerate(shifts):
                out = out + pltpu.roll(y, sh, 1) * tabs[1 + k][...]
            y = out
        o_ref[:, c * cw:(c + 1) * cw] = y.astype(o_ref.dtype)


def _r_spec(tm):
    return pl.BlockSpec((tm, LANES), lambda i, j: (i, 0))


def mm_cast(a, b, tm, tn, out_dtype=BF16):
    m, k = a.shape
    n_out = b.shape[1]
    return pl.pallas_call(
        _mm_cast_kernel,
        grid=(m // tm, n_out // tn),
        in_specs=[pl.BlockSpec((tm, k), lambda i, j: (i, 0)),
                  pl.BlockSpec((k, tn), lambda i, j: (0, j))],
        out_specs=pl.BlockSpec((tm, tn), lambda i, j: (i, j)),
        out_shape=jax.ShapeDtypeStruct((m, n_out), out_dtype),
        compiler_params=_params(),
        name="mm_cast",
    )(a, b)


def mm_swiglu(xg, r, w_gu, layer, tm, tn):
    m, k = xg.shape
    d_ff = w_gu.shape[2] // 2
    off = d_ff // tn
    return pl.pallas_call(
        _mm_swiglu_kernel,
        grid=(m // tm, d_ff // tn),
        in_specs=[pl.BlockSpec((tm, k), lambda i, j: (i, 0)),
                  _r_spec(tm),
                  pl.BlockSpec((None, k, tn), lambda i, j: (layer, 0, j)),
                  pl.BlockSpec((None, k, tn), lambda i, j: (layer, 0, j + off))],
        out_specs=pl.BlockSpec((tm, tn), lambda i, j: (i, j)),
        out_shape=jax.ShapeDtypeStruct((m, d_ff), BF16),
        compiler_params=_params(),
        name="mm_swiglu",
    )(xg, r, w_gu, w_gu)


def mm_residual(a_list, w, layer, res, scale, g_next, tm, tn):
    m = a_list[0].shape[0]
    n_out = w.shape[2]
    in_specs = [pl.BlockSpec((tm, a.shape[1]), lambda i, j: (i, 0)) for a in a_list]
    row = 0
    for a in a_list:
        ka = a.shape[1]
        assert row % ka == 0
        in_specs.append(pl.BlockSpec((None, ka, tn), functools.partial(lambda i, j, rb: (layer, rb, j), rb=row // ka)))
        row += ka
    assert row == w.shape[1]
    tile_spec = pl.BlockSpec((tm, tn), lambda i, j: (i, j))
    in_specs.append(tile_spec)
    args = list(a_list) + [w] * len(a_list) + [res]
    emit_next = g_next is not None
    if emit_next:
        in_specs.append(pl.BlockSpec((1, tn), lambda i, j: (0, j)))
        args.append(g_next.reshape(1, n_out))
        out_specs = [tile_spec, tile_spec, pl.BlockSpec((tm, LANES), lambda i, j: (i, j))]
        out_shape = [jax.ShapeDtypeStruct((m, n_out), F32), jax.ShapeDtypeStruct((m, n_out), BF16),
                     jax.ShapeDtypeStruct((m, (n_out // tn) * LANES), F32)]
    else:
        out_specs = tile_spec
        out_shape = jax.ShapeDtypeStruct((m, n_out), F32)
    return pl.pallas_call(
        functools.partial(_mm_residual_kernel, n_a=len(a_list), scale=scale, emit_next=emit_next),
        grid=(m // tm, n_out // tn),
        in_specs=in_specs,
        out_specs=out_specs,
        out_shape=out_shape,
        compiler_params=_params(),
        name="mm_residual",
    )(*args)


def mm_headnorm(a, b, g, cw, count, tm, tn, rope=None, seq_tiles=1, r=None):
    m, k = a.shape
    n_out = b.shape[1]
    in_specs = [pl.BlockSpec((tm, k), lambda i, j: (i, 0)),
                pl.BlockSpec((k, tn), lambda i, j: (0, j))]
    args = [a, b]
    if r is not None:
        in_specs.append(_r_spec(tm))
        args.append(r)
    in_specs.append(pl.BlockSpec((1, cw), lambda i, j: (0, 0)))
    args.append(g.reshape(1, cw))
    shifts = ()
    if rope is not None:
        shifts, tabs = rope
        for t in tabs:
            in_specs.append(pl.BlockSpec((tm, cw), lambda i, j: (i % seq_tiles, 0)))
            args.append(t)
    kern = functools.partial(_mm_headnorm_kernel, cw=cw, inv_count=1.0 / count, shifts=tuple(shifts),
                             scaled=r is not None)
    return pl.pallas_call(
        kern,
        grid=(m // tm, n_out // tn),
        in_specs=in_specs,
        out_specs=pl.BlockSpec((tm, tn), lambda i, j: (i, j)),
        out_shape=jax.ShapeDtypeStruct((m, n_out), BF16),
        compiler_params=_params(),
        name="mm_headnorm",
    )(*args)


def _attn_kernel(*refs, nparts, tk, nkv):
    q_refs = refs[:nparts]
    k_refs = refs[nparts:2 * nparts]
    v_ref = refs[2 * nparts]
    o_ref = refs[2 * nparts + 1]
    if nparts > 1:
        q = jnp.concatenate([r[...] for r in q_refs], axis=-1)
    else:
        q = q_refs[0][...]
    tq = q.shape[0]
    dv = v_ref.shape[-1]

    def body(t, carry):
        m, l, acc = carry
        off = pl.multiple_of(t * tk, tk)
        if nparts > 1:
            k = jnp.concatenate([r[pl.ds(off, tk), :] for r in k_refs], axis=-1)
        else:
            k = k_refs[0][pl.ds(off, tk), :]
        s = lax.dot_general(q, k, (((1,), (1,)), ((), ())), preferred_element_type=F32)
        m_new = jnp.maximum(m, jnp.max(s, axis=-1, keepdims=True))
        p = jnp.exp(s - m_new)
        alpha = jnp.exp(m - m_new)
        l = alpha * l + jnp.sum(p, axis=-1, keepdims=True)
        acc = alpha * acc + jnp.dot(p.astype(BF16), v_ref[pl.ds(off, tk), :], preferred_element_type=F32)
        return m_new, l, acc

    init = (jnp.full((tq, 1), -jnp.inf, F32), jnp.zeros((tq, 1), F32), jnp.zeros((tq, dv), F32))
    if nkv == 1:
        m, l, acc = body(0, init)
    else:
        m, l, acc = lax.fori_loop(0, nkv, body, init)
    o_ref[...] = (acc / l).astype(o_ref.dtype)


def attention(qs, ks, v, n_heads, group, dq, dv, tq, tk):
    b, s, _ = qs[0].shape
    skv = v.shape[1]
    nparts = len(qs)
    in_specs = [pl.BlockSpec((None, tq, dq), lambda bi, h, qi: (bi, qi, h)) for _ in qs]
    for k in ks:
        if k.shape[2] == dq:
            in_specs.append(pl.BlockSpec((None, skv, dq), lambda bi, h, qi: (bi, 0, 0)))
        else:
            in_specs.append(pl.BlockSpec((None, skv, dq), lambda bi, h, qi: (bi, 0, h // group)))
    in_specs.append(pl.BlockSpec((None, skv, dv), lambda bi, h, qi: (bi, 0, h // group)))
    kern = functools.partial(_attn_kernel, nparts=nparts, tk=tk, nkv=skv // tk)
    return pl.pallas_call(
        kern,
        grid=(b, n_heads, s // tq),
        in_specs=in_specs,
        out_specs=pl.BlockSpec((None, tq, dv), lambda bi, h, qi: (bi, qi, h)),
        out_shape=jax.ShapeDtypeStruct((b, s, n_heads * dv), BF16),
        compiler_params=_params(),
        name="attention",
    )(*qs, *ks, v)


def _attn_t_kernel(*refs, nparts, tk, nkv):
    q_refs = refs[:nparts]
    k_refs = refs[nparts:2 * nparts]
    vt_ref = refs[2 * nparts]
    o_ref = refs[2 * nparts + 1]
    scratch = refs[2 * nparts + 2:]
    dv = vt_ref.shape[1]
    first_q_tile = pl.program_id(2) == 0
    if nparts > 1:
        kcat_ref, vx_ref, s_ref, m_ref, acc_ref = scratch
        q = jnp.concatenate([r[...] for r in q_refs], axis=-1)

        @pl.when(first_q_tile)
        def _():
            for i, r in enumerate(k_refs):
                kcat_ref[:, i * HEAD_DIM:(i + 1) * HEAD_DIM] = r[...]
        k_ref = kcat_ref
    else:
        vx_ref, s_ref, m_ref, acc_ref = scratch
        q = q_refs[0][...]
        k_ref = k_refs[0]

    @pl.when(first_q_tile)
    def _():
        vx_ref[:, :dv, :] = vt_ref[...]
        vx_ref[:, dv:, :] = jnp.ones((nkv, ONES_ROWS, tk), BF16)

    qt = q.astype(F32).T.astype(BF16)

    def qk(c, slot):
        off = pl.multiple_of(c * tk, tk)
        s_ref[slot] = jnp.dot(k_ref[pl.ds(off, tk), :], qt, preferred_element_type=F32)

    def softmax_pv(c, slot):
        s = s_ref[slot]
        m_old = m_ref[...]
        m_new = jnp.maximum(m_old, jnp.max(s, axis=0, keepdims=True))
        alpha = jnp.exp2(m_old - m_new)
        p = jnp.exp2(s - m_new)
        m_ref[...] = m_new
        acc_ref[...] = alpha * acc_ref[...] + jnp.dot(vx_ref[c], p.astype(BF16), preferred_element_type=F32)

    m_ref[...] = jnp.full(m_ref.shape, -jnp.inf, F32)
    acc_ref[...] = jnp.zeros(acc_ref.shape, F32)
    qk(0, 0)

    def pair(i, carry):
        c = 2 * i
        qk(c + 1, 1)
        softmax_pv(c, 0)
        qk(c + 2, 0)
        softmax_pv(c + 1, 1)
        return carry

    lax.fori_loop(0, nkv // 2 - 1, pair, 0)
    qk(nkv - 1, 1)
    softmax_pv(nkv - 2, 0)
    softmax_pv(nkv - 1, 1)
    o_ref[...] = (acc_ref[:dv, :] * (1.0 / acc_ref[dv:dv + 1, :])).T.astype(o_ref.dtype)


def attention_t(qs, ks, vt, n_heads, group, tq):
    b, s, _ = qs[0].shape
    skv = ks[0].shape[1]
    nkv, tk = vt.shape[1], vt.shape[3]
    n_kv = n_heads // group
    dv = vt.shape[2] // n_kv
    nparts = len(qs)
    assert nkv % 2 == 0 and nkv * tk == skv
    in_specs = [pl.BlockSpec((None, tq, HEAD_DIM), lambda bi, h, qi: (bi, qi, h)) for _ in qs]
    for k in ks:
        if k.shape[2] == n_kv * HEAD_DIM:
            in_specs.append(pl.BlockSpec((None, skv, HEAD_DIM), lambda bi, h, qi: (bi, 0, h // group)))
        else:
            in_specs.append(pl.BlockSpec((None, skv, HEAD_DIM), lambda bi, h, qi: (bi, 0, 0)))
    in_specs.append(pl.BlockSpec((None, nkv, dv, tk), lambda bi, h, qi: (bi, 0, h // group, 0)))
    scratch = []
    if nparts > 1:
        scratch.append(pltpu.VMEM((skv, nparts * HEAD_DIM), BF16))
    scratch += [pltpu.VMEM((nkv, dv + ONES_ROWS, tk), BF16), pltpu.VMEM((2, tk, tq), F32),
                pltpu.VMEM((1, tq), F32), pltpu.VMEM((dv + ONES_ROWS, tq), F32)]
    kern = functools.partial(_attn_t_kernel, nparts=nparts, tk=tk, nkv=nkv)
    return pl.pallas_call(
        kern,
        grid=(b, n_heads, s // tq),
        in_specs=in_specs,
        out_specs=pl.BlockSpec((None, tq, dv), lambda bi, h, qi: (bi, qi, h)),
        out_shape=jax.ShapeDtypeStruct((b, s, n_heads * dv), BF16),
        scratch_shapes=scratch,
        compiler_params=_params(),
        name="attention_t",
    )(*qs, *ks, vt)


def _mm_vt_kernel(*refs, scaled):
    a_ref, wt_ref, o_ref = refs[0], refs[1], refs[-1]
    out = lax.dot_general(wt_ref[...], a_ref[...], (((1,), (1,)), ((), ())), preferred_element_type=F32)
    if scaled:
        out = out * refs[2][...].T[:1, :]
    o_ref[...] = out.astype(o_ref.dtype)


def mm_vt(a, wt, tk, tn, r=None):
    m, k = a.shape
    cols = wt.shape[0]
    in_specs = [pl.BlockSpec((tk, k), lambda i, j: (i, 0)),
                pl.BlockSpec((tn, k), lambda i, j: (j, 0))]
    args = [a, wt]
    if r is not None:
        in_specs.append(_r_spec(tk))
        args.append(r)
    return pl.pallas_call(
        functools.partial(_mm_vt_kernel, scaled=r is not None),
        grid=(m // tk, cols // tn),
        in_specs=in_specs,
        out_specs=pl.BlockSpec((None, tn, tk), lambda i, j: (i, j, 0)),
        out_shape=jax.ShapeDtypeStruct((m // tk, cols, tk), BF16),
        compiler_params=_params(),
        name="mm_vt",
    )(*args)


def _deinterleave(w, width):
    lead = w.shape[:-1]
    n = w.shape[-1] // width
    w = w.reshape(lead + (n, width // 2, 2))
    w = jnp.swapaxes(w, -1, -2)
    return w.reshape(lead + (n * width,))


def _pad_groups(w, width, to):
    lead = w.shape[:-1]
    n = w.shape[-1] // width
    w = w.reshape(lead + (n, width))
    w = jnp.pad(w, [(0, 0)] * len(lead) + [(0, 0), (0, to - width)])
    return w.reshape(lead + (n * to,))


def _rope_angles(seq_len, rot_dim):
    rows = seq_len // GRID_W
    row, col = jnp.meshgrid(jnp.arange(rows), jnp.arange(GRID_W), indexing="ij")
    row = row.reshape(-1).astype(F32)
    col = col.reshape(-1).astype(F32)
    axis_dim = rot_dim // 2
    inv_freq = ROPE_THETA ** (-jnp.arange(0, axis_dim, 2, dtype=F32) / axis_dim)
    ang = jnp.concatenate([row[:, None] * inv_freq, col[:, None] * inv_freq], axis=-1)
    return jnp.cos(ang), jnp.sin(ang)


def _rope_tables(seq_len, rot_dim, width):
    cos, sin = _rope_angles(seq_len, rot_dim)
    half = rot_dim // 2
    z = jnp.zeros((seq_len, width - rot_dim), F32)
    zh = jnp.zeros_like(sin)
    c = jnp.concatenate([cos, cos, z], axis=-1)
    if rot_dim == width:
        return (half,), [c, jnp.concatenate([-sin, sin], axis=-1)]
    s_a = jnp.concatenate([-sin, zh, z], axis=-1)
    s_b = jnp.concatenate([zh, sin, z], axis=-1)
    return (width - half, half), [c, s_a, s_b]


def kernel(x, mem, ffn1_norm, ffn1_w_gu, ffn1_w_down, mix_norm, w_o, mem_norm, w_mem_kv, mem_q_norm, mem_k_norm, ffn2_norm, ffn2_w_gu, ffn2_w_down, a_w_in, a_q_a_norm, a_kv_a_norm, a_w_q_b, a_w_kv_b, a_q_nope_norm, a_q_pe_norm, a_k_nope_norm, a_k_pe_norm, b_w_in, b_q_norm, b_k_norm):
    b, s, d = x.shape
    n = b * s
    m_tok = mem.shape[1]
    depth = ffn1_norm.shape[0]
    hd = HEAD_DIM
    mem_w = w_mem_kv.shape[2] // 2
    mem_hd = mem_w // MEM_HEADS
    tok_w = w_o.shape[1] - mem_w
    n_heads = tok_w // hd
    q_lora = a_q_a_norm.shape[1]
    kv_lora = a_kv_a_norm.shape[1]
    n_kv = n_heads // B_GROUP

    tm = min(1024, s)
    seq_tiles = s // tm
    tq = min(1024, s)
    tk = min(512, s // 2)
    tm_mem = min(512, b * m_tok)
    tm_norm = min(256, s)

    def tile(width, pref):
        t = min(pref, width)
        while width % t:
            t -= hd
        return t

    rope_a = _rope_tables(s, A_ROPE, hd)
    rope_b = _rope_tables(s, hd, hd)
    scale_a = float((hd + A_ROPE) ** -0.5) * LOG2E
    scale_b = float(hd ** -0.5) * LOG2E
    scale_m = float(mem_hd ** -0.5)

    xf = x.reshape(n, d)
    memf = mem.reshape(b * m_tok, d)
    w_gu1, w_gu2 = ffn1_w_gu.astype(BF16), ffn2_w_gu.astype(BF16)
    w_dn1, w_dn2 = ffn1_w_down.astype(BF16), ffn2_w_down.astype(BF16)
    w_ob = w_o.astype(BF16)
    d_ff = w_gu1.shape[2] // 2
    r3 = lambda t: t.reshape(b, s, t.shape[-1])

    def produce(a_list, w, scale, g_next):
        x_new, xg, ss = mm_residual(a_list, w, i, xf, scale, g_next, tm, tile(d, 512))
        return x_new, xg, row_scale(ss, d, tm)

    xg, r = scale_prep(xf, ffn1_norm[0], tm_norm)
    for i in range(depth):
        act = mm_swiglu(xg, r, w_gu1, i, tm, tile(d_ff, 512))
        xf, xg, r = produce([act], w_dn1, 0.5, mix_norm[i])
        proj = functools.partial(mm_headnorm, r=r)
        j = i // N_MIXERS
        tq_a, tk_a = ((2048, 512), (1024, 1024), (512, 512), (1024, 256))[i % 4] if s >= 8192 else (tq, tk)
        if i % N_MIXERS == 0:
            w_in = a_w_in[j]
            o1, o2, o3 = q_lora, q_lora + kv_lora, q_lora + kv_lora + A_ROPE
            w_cq = w_in[:, :o1].astype(BF16)
            w_ckv = w_in[:, o1:o2].astype(BF16)
            w_kpe = _pad_groups(_deinterleave(w_in[:, o2:o3], A_ROPE), A_ROPE, hd).astype(BF16)
            w_qm = w_in[:, o3:].astype(BF16)
            wq = a_w_q_b[j].reshape(q_lora, n_heads, hd + A_ROPE)
            w_qn = wq[:, :, :hd].reshape(q_lora, n_heads * hd).astype(BF16)
            w_qp = _pad_groups(_deinterleave(wq[:, :, hd:].reshape(q_lora, n_heads * A_ROPE), A_ROPE),
                               A_ROPE, hd).astype(BF16)
            wkv = a_w_kv_b[j].reshape(kv_lora, n_heads, 2 * hd)
            w_kn = wkv[:, :, :hd].reshape(kv_lora, n_heads * hd).astype(BF16)
            w_v = wkv[:, :, hd:].reshape(kv_lora, n_heads * hd).astype(BF16)
            g_qpe = _pad_groups(_deinterleave(a_q_pe_norm[j], A_ROPE), A_ROPE, hd) * scale_a
            g_kpe = _pad_groups(_deinterleave(a_k_pe_norm[j], A_ROPE), A_ROPE, hd)

            cqn = proj(xg, w_cq, a_q_a_norm[j], q_lora, q_lora, tm, q_lora)
            ckvn = proj(xg, w_ckv, a_kv_a_norm[j], kv_lora, kv_lora, tm, kv_lora)
            kpe = proj(xg, w_kpe, g_kpe, hd, A_ROPE, tm, hd, rope=rope_a, seq_tiles=seq_tiles)
            mq = proj(xg, w_qm, mem_q_norm[i] * scale_m, mem_hd, mem_hd, tm, tile(mem_w, 1024))
            tw = tile(n_heads * hd, 1024)
            qn = mm_headnorm(cqn, w_qn, a_q_nope_norm[j] * scale_a, hd, hd, tm, tw)
            qp = mm_headnorm(cqn, w_qp, g_qpe, hd, A_ROPE, tm, tw, rope=rope_a, seq_tiles=seq_tiles)
            kn = mm_headnorm(ckvn, w_kn, a_k_nope_norm[j], hd, hd, tm, tw)
            vt = mm_vt(ckvn, w_v.T, tk_a, tw).reshape(b, s // tk_a, n_heads * hd, tk_a)
            tok = attention_t([r3(qn), r3(qp)], [r3(kn), r3(kpe)], vt, n_heads, 1, tq_a)
        else:
            w_in = b_w_in[j]
            qw, kw = n_heads * hd, n_kv * hd
            w_q = _deinterleave(w_in[:, :qw], hd).astype(BF16)
            w_k = _deinterleave(w_in[:, qw:qw + kw], hd).astype(BF16)
            w_v = w_in[:, qw + kw:qw + 2 * kw].astype(BF16)
            w_qm = w_in[:, qw + 2 * kw:].astype(BF16)
            g_q = _deinterleave(b_q_norm[j], hd) * scale_b
            g_k = _deinterleave(b_k_norm[j], hd)
            q = proj(xg, w_q, g_q, hd, hd, tm, tile(qw, 1024), rope=rope_b, seq_tiles=seq_tiles)
            k = proj(xg, w_k, g_k, hd, hd, tm, tile(kw, 1024), rope=rope_b, seq_tiles=seq_tiles)
            vt = mm_vt(xg, w_v.T, tk_a, tile(kw, 1024), r=r).reshape(b, s // tk_a, kw, tk_a)
            mq = proj(xg, w_qm, mem_q_norm[i] * scale_m, mem_hd, mem_hd, tm, tile(mem_w, 1024))
            tok = attention_t([r3(q)], [r3(k)], vt, n_heads, B_GROUP, tq_a)

        memn = rmsnorm(memf, mem_norm[i], tm_mem)
        w_mkv = w_mem_kv[i].astype(BF16)
        mk = mm_headnorm(memn, w_mkv[:, :mem_w], mem_k_norm[i], mem_hd, mem_hd, tm_mem, tile(mem_w, 1024))
        mv = mm_cast(memn, w_mkv[:, mem_w:], tm_mem, tile(mem_w, 1024))
        mo = attention([mq.reshape(b, s, mem_w)], [mk.reshape(b, m_tok, mem_w)], mv.reshape(b, m_tok, mem_w),
                       MEM_HEADS, 1, mem_hd, mem_hd, tq, m_tok)
        xf, xg, r = produce([tok.reshape(n, tok_w), mo.reshape(n, mem_w)], w_ob, 1.0, ffn2_norm[i])
        act = mm_swiglu(xg, r, w_gu2, i, tm, tile(d_ff, 512))
        if i + 1 < depth:
            xf, xg, r = produce([act], w_dn2, 0.5, ffn1_norm[i + 1])
        else:
            xf = mm_residual([act], w_dn2, i, xf, 0.5, None, tm, tile(d, 512))
    return xf.reshape(b, s, d)
```

```python
import functools

import jax
import jax.numpy as jnp
from jax import lax
from jax.experimental import pallas as pl
from jax.experimental.pallas import tpu as pltpu

EPS = 1e-6
GRID_W = 64
ROPE_THETA = 10000.0
LANES = 128
HEAD_DIM = LANES
A_ROPE = 64
MEM_HEADS = 4
LOG2E = 1.4426950408889634
N_MIXERS = 2
B_GROUP = 4
ONES_ROWS = 16
V7X_VMEM_BYTES = 64 * 1024 * 1024
VMEM_LIMIT = V7X_VMEM_BYTES * 7 // 8

F32 = jnp.float32
BF16 = jnp.bfloat16


def _params():
    return pltpu.CompilerParams(vmem_limit_bytes=VMEM_LIMIT)


def _rmsnorm_kernel(x_ref, g_ref, o_ref):
    x = x_ref[...]
    ms = jnp.mean(x * x, axis=-1, keepdims=True)
    o_ref[...] = (x * lax.rsqrt(ms + EPS) * g_ref[...]).astype(o_ref.dtype)


def rmsnorm(x, g, tm):
    m, d = x.shape
    return pl.pallas_call(
        _rmsnorm_kernel,
        grid=(m // tm,),
        in_specs=[pl.BlockSpec((tm, d), lambda i: (i, 0)),
                  pl.BlockSpec((1, d), lambda i: (0, 0))],
        out_specs=pl.BlockSpec((tm, d), lambda i: (i, 0)),
        out_shape=jax.ShapeDtypeStruct((m, d), BF16),
        compiler_params=_params(),
        name="rmsnorm",
    )(x, g.reshape(1, d))


def _row_scale_kernel(ss_ref, r_ref, *, inv_d):
    ss = ss_ref[...]
    tot = ss[:, :LANES]
    for c in range(1, ss.shape[1] // LANES):
        tot = tot + ss[:, c * LANES:(c + 1) * LANES]
    r_ref[...] = lax.rsqrt(tot * inv_d + EPS)


def row_scale(ss, d_norm, tm):
    m, w = ss.shape
    return pl.pallas_call(
        functools.partial(_row_scale_kernel, inv_d=1.0 / d_norm),
        grid=(m // tm,),
        in_specs=[pl.BlockSpec((tm, w), lambda i: (i, 0))],
        out_specs=pl.BlockSpec((tm, LANES), lambda i: (i, 0)),
        out_shape=jax.ShapeDtypeStruct((m, LANES), F32),
        compiler_params=_params(),
        name="row_scale",
    )(ss)


def _widen(r, width):
    return r if width == LANES else jnp.tile(r, (1, width // LANES))


def _scale_prep_kernel(x_ref, g_ref, xg_ref, r_ref):
    x = x_ref[...]
    xg_ref[...] = (x * g_ref[...]).astype(BF16)
    r = lax.rsqrt(jnp.mean(x * x, axis=-1, keepdims=True) + EPS)
    r_ref[...] = jnp.broadcast_to(r, r_ref.shape)


def scale_prep(x, g, tm):
    m, d = x.shape
    return pl.pallas_call(
        _scale_prep_kernel,
        grid=(m // tm,),
        in_specs=[pl.BlockSpec((tm, d), lambda i: (i, 0)),
                  pl.BlockSpec((1, d), lambda i: (0, 0))],
        out_specs=[pl.BlockSpec((tm, d), lambda i: (i, 0)),
                   pl.BlockSpec((tm, LANES), lambda i: (i, 0))],
        out_shape=[jax.ShapeDtypeStruct((m, d), BF16), jax.ShapeDtypeStruct((m, LANES), F32)],
        compiler_params=_params(),
        name="scale_prep",
    )(x, g.reshape(1, d))


def _mm_cast_kernel(a_ref, b_ref, o_ref):
    acc = jnp.dot(a_ref[...], b_ref[...], preferred_element_type=F32)
    o_ref[...] = acc.astype(o_ref.dtype)


def _mm_swiglu_kernel(a_ref, r_ref, bg_ref, bu_ref, o_ref):
    a = a_ref[...]
    r = _widen(r_ref[...], o_ref.shape[1])
    g = jnp.dot(a, bg_ref[...], preferred_element_type=F32) * r
    u = jnp.dot(a, bu_ref[...], preferred_element_type=F32) * r
    o_ref[...] = (g * (1.0 / (1.0 + jnp.exp(-g))) * u).astype(o_ref.dtype)


def _mm_residual_kernel(*refs, n_a, scale, emit_next):
    a_refs = refs[:n_a]
    b_refs = refs[n_a:2 * n_a]
    r_ref = refs[2 * n_a]
    acc = jnp.dot(a_refs[0][...], b_refs[0][...], preferred_element_type=F32)
    for a_ref, b_ref in zip(a_refs[1:], b_refs[1:]):
        acc = acc + jnp.dot(a_ref[...], b_ref[...], preferred_element_type=F32)
    x = r_ref[...] + (acc if scale == 1.0 else scale * acc)
    if emit_next:
        g_ref = refs[2 * n_a + 1]
        o_ref, xg_ref, ss_ref = refs[-3:]
        xg_ref[...] = (x * g_ref[...]).astype(BF16)
        ss_ref[...] = jnp.broadcast_to(jnp.sum(x * x, axis=-1, keepdims=True), ss_ref.shape)
    else:
        o_ref = refs[-1]
    o_ref[...] = x


def _mm_headnorm_kernel(*refs, cw, inv_count, shifts, scaled):
    a_ref, b_ref = refs[0], refs[1]
    nxt = 2
    acc = jnp.dot(a_ref[...], b_ref[...], preferred_element_type=F32)
    if scaled:
        acc = acc * _widen(refs[2][...], acc.shape[1])
        nxt = 3
    g = refs[nxt][...]
    tabs = refs[nxt + 1:-1]
    o_ref = refs[-1]
    for c in range(acc.shape[1] // cw):
        x = acc[:, c * cw:(c + 1) * cw]
        ms = jnp.sum(x * x, axis=-1, keepdims=True) * inv_count
        y = x * lax.rsqrt(ms + EPS) * g
        if shifts:
            out = y * tabs[0][...]
            for k, sh in enumerate(shifts):
                out = out + pltpu.roll(y, sh, 1) * tabs[1 + k][...]
            y = out
        o_ref[:, c * cw:(c + 1) * cw] = y.astype(o_ref.dtype)


def _r_spec(tm):
    return pl.BlockSpec((tm, LANES), lambda i, j: (i, 0))


def mm_cast(a, b, tm, tn, out_dtype=BF16):
    m, k = a.shape
    n_out = b.shape[1]
    return pl.pallas_call(
        _mm_cast_kernel,
        grid=(m // tm, n_out // tn),
        in_specs=[pl.BlockSpec((tm, k), lambda i, j: (i, 0)),
                  pl.BlockSpec((k, tn), lambda i, j: (0, j))],
        out_specs=pl.BlockSpec((tm, tn), lambda i, j: (i, j)),
        out_shape=jax.ShapeDtypeStruct((m, n_out), out_dtype),
        compiler_params=_params(),
        name="mm_cast",
    )(a, b)


def mm_swiglu(xg, r, w_gu, layer, tm, tn):
    m, k = xg.shape
    d_ff = w_gu.shape[2] // 2
    off = d_ff // tn
    return pl.pallas_call(
        _mm_swiglu_kernel,
        grid=(m // tm, d_ff // tn),
        in_specs=[pl.BlockSpec((tm, k), lambda i, j: (i, 0)),
                  _r_spec(tm),
                  pl.BlockSpec((None, k, tn), lambda i, j: (layer, 0, j)),
                  pl.BlockSpec((None, k, tn), lambda i, j: (layer, 0, j + off))],
        out_specs=pl.BlockSpec((tm, tn), lambda i, j: (i, j)),
        out_shape=jax.ShapeDtypeStruct((m, d_ff), BF16),
        compiler_params=_params(),
        name="mm_swiglu",
    )(xg, r, w_gu, w_gu)


def mm_residual(a_list, w, layer, res, scale, g_next, tm, tn):
    m = a_list[0].shape[0]
    n_out = w.shape[2]
    in_specs = [pl.BlockSpec((tm, a.shape[1]), lambda i, j: (i, 0)) for a in a_list]
    row = 0
    for a in a_list:
        ka = a.shape[1]
        assert row % ka == 0
        in_specs.append(pl.BlockSpec((None, ka, tn), functools.partial(lambda i, j, rb: (layer, rb, j), rb=row // ka)))
        row += ka
    assert row == w.shape[1]
    tile_spec = pl.BlockSpec((tm, tn), lambda i, j: (i, j))
    in_specs.append(tile_spec)
    args = list(a_list) + [w] * len(a_list) + [res]
    emit_next = g_next is not None
    if emit_next:
        in_specs.append(pl.BlockSpec((1, tn), lambda i, j: (0, j)))
        args.append(g_next.reshape(1, n_out))
        out_specs = [tile_spec, tile_spec, pl.BlockSpec((tm, LANES), lambda i, j: (i, j))]
        out_shape = [jax.ShapeDtypeStruct((m, n_out), F32), jax.ShapeDtypeStruct((m, n_out), BF16),
                     jax.ShapeDtypeStruct((m, (n_out // tn) * LANES), F32)]
    else:
        out_specs = tile_spec
        out_shape = jax.ShapeDtypeStruct((m, n_out), F32)
    return pl.pallas_call(
        functools.partial(_mm_residual_kernel, n_a=len(a_list), scale=scale, emit_next=emit_next),
        grid=(m // tm, n_out // tn),
        in_specs=in_specs,
        out_specs=out_specs,
        out_shape=out_shape,
        compiler_params=_params(),
        name="mm_residual",
    )(*args)


def mm_headnorm(a, b, g, cw, count, tm, tn, rope=None, seq_tiles=1, r=None):
    m, k = a.shape
    n_out = b.shape[1]
    in_specs = [pl.BlockSpec((tm, k), lambda i, j: (i, 0)),
                pl.BlockSpec((k, tn), lambda i, j: (0, j))]
    args = [a, b]
    if r is not None:
        in_specs.append(_r_spec(tm))
        args.append(r)
    in_specs.append(pl.BlockSpec((1, cw), lambda i, j: (0, 0)))
    args.append(g.reshape(1, cw))
    shifts = ()
    if rope is not None:
        shifts, tabs = rope
        for t in tabs:
            in_specs.append(pl.BlockSpec((tm, cw), lambda i, j: (i % seq_tiles, 0)))
            args.append(t)
    kern = functools.partial(_mm_headnorm_kernel, cw=cw, inv_count=1.0 / count, shifts=tuple(shifts),
                             scaled=r is not None)
    return pl.pallas_call(
        kern,
        grid=(m // tm, n_out // tn),
        in_specs=in_specs,
        out_specs=pl.BlockSpec((tm, tn), lambda i, j: (i, j)),
        out_shape=jax.ShapeDtypeStruct((m, n_out), BF16),
        compiler_params=_params(),
        name="mm_headnorm",
    )(*args)


def _attn_kernel(*refs, nparts, tk, nkv):
    q_refs = refs[:nparts]
    k_refs = refs[nparts:2 * nparts]
    v_ref = refs[2 * nparts]
    o_ref = refs[2 * nparts + 1]
    if nparts > 1:
        q = jnp.concatenate([r[...] for r in q_refs], axis=-1)
    else:
        q = q_refs[0][...]
    tq = q.shape[0]
    dv = v_ref.shape[-1]

    def body(t, carry):
        m, l, acc = carry
        off = pl.multiple_of(t * tk, tk)
        if nparts > 1:
            k = jnp.concatenate([r[pl.ds(off, tk), :] for r in k_refs], axis=-1)
        else:
            k = k_refs[0][pl.ds(off, tk), :]
        s = lax.dot_general(q, k, (((1,), (1,)), ((), ())), preferred_element_type=F32)
        m_new = jnp.maximum(m, jnp.max(s, axis=-1, keepdims=True))
        p = jnp.exp(s - m_new)
        alpha = jnp.exp(m - m_new)
        l = alpha * l + jnp.sum(p, axis=-1, keepdims=True)
        acc = alpha * acc + jnp.dot(p.astype(BF16), v_ref[pl.ds(off, tk), :], preferred_element_type=F32)
        return m_new, l, acc

    init = (jnp.full((tq, 1), -jnp.inf, F32), jnp.zeros((tq, 1), F32), jnp.zeros((tq, dv), F32))
    if nkv == 1:
        m, l, acc = body(0, init)
    else:
        m, l, acc = lax.fori_loop(0, nkv, body, init)
    o_ref[...] = (acc / l).astype(o_ref.dtype)


def attention(qs, ks, v, n_heads, group, dq, dv, tq, tk):
    b, s, _ = qs[0].shape
    skv = v.shape[1]
    nparts = len(qs)
    in_specs = [pl.BlockSpec((None, tq, dq), lambda bi, h, qi: (bi, qi, h)) for _ in qs]
    for k in ks:
        if k.shape[2] == dq:
            in_specs.append(pl.BlockSpec((None, skv, dq), lambda bi, h, qi: (bi, 0, 0)))
        else:
            in_specs.append(pl.BlockSpec((None, skv, dq), lambda bi, h, qi: (bi, 0, h // group)))
    in_specs.append(pl.BlockSpec((None, skv, dv), lambda bi, h, qi: (bi, 0, h // group)))
    kern = functools.partial(_attn_kernel, nparts=nparts, tk=tk, nkv=skv // tk)
    return pl.pallas_call(
        kern,
        grid=(b, n_heads, s // tq),
        in_specs=in_specs,
        out_specs=pl.BlockSpec((None, tq, dv), lambda bi, h, qi: (bi, qi, h)),
        out_shape=jax.ShapeDtypeStruct((b, s, n_heads * dv), BF16),
        compiler_params=_params(),
        name="attention",
    )(*qs, *ks, v)


def _attn_t_kernel(*refs, nparts, tk, nkv):
    q_refs = refs[:nparts]
    k_refs = refs[nparts:2 * nparts]
    vt_ref = refs[2 * nparts]
    o_ref = refs[2 * nparts + 1]
    scratch = refs[2 * nparts + 2:]
    dv = vt_ref.shape[1]
    first_q_tile = pl.program_id(2) == 0
    if nparts > 1:
        kcat_ref, vx_ref, s_ref, m_ref, acc_ref = scratch
        q = jnp.concatenate([r[...] for r in q_refs], axis=-1)

        @pl.when(first_q_tile)
        def _():
            for i, r in enumerate(k_refs):
                kcat_ref[:, i * HEAD_DIM:(i + 1) * HEAD_DIM] = r[...]
        k_ref = kcat_ref
    else:
        vx_ref, s_ref, m_ref, acc_ref = scratch
        q = q_refs[0][...]
        k_ref = k_refs[0]

    @pl.when(first_q_tile)
    def _():
        vx_ref[:, :dv, :] = vt_ref[...]
        vx_ref[:, dv:, :] = jnp.ones((nkv, ONES_ROWS, tk), BF16)

    qt = q.astype(F32).T.astype(BF16)

    def qk(c, slot):
        off = pl.multiple_of(c * tk, tk)
        s_ref[slot] = jnp.dot(k_ref[pl.ds(off, tk), :], qt, preferred_element_type=F32)

    def softmax_pv(c, slot):
        s = s_ref[slot]
        m_old = m_ref[...]
        m_new = jnp.maximum(m_old, jnp.max(s, axis=0, keepdims=True))
        alpha = jnp.exp2(m_old - m_new)
        p = jnp.exp2(s - m_new)
        m_ref[...] = m_new
        acc_ref[...] = alpha * acc_ref[...] + jnp.dot(vx_ref[c], p.astype(BF16), preferred_element_type=F32)

    m_ref[...] = jnp.full(m_ref.shape, -jnp.inf, F32)
    acc_ref[...] = jnp.zeros(acc_ref.shape, F32)
    qk(0, 0)

    def pair(i, carry):
        c = 2 * i
        qk(c + 1, 1)
        softmax_pv(c, 0)
        qk(c + 2, 0)
        softmax_pv(c + 1, 1)
        return carry

    lax.fori_loop(0, nkv // 2 - 1, pair, 0)
    qk(nkv - 1, 1)
    softmax_pv(nkv - 2, 0)
    softmax_pv(nkv - 1, 1)
    o_ref[...] = (acc_ref[:dv, :] * (1.0 / acc_ref[dv:dv + 1, :])).T.astype(o_ref.dtype)


def attention_t(qs, ks, vt, n_heads, group, tq):
    b, s, _ = qs[0].shape
    skv = ks[0].shape[1]
    nkv, tk = vt.shape[1], vt.shape[3]
    n_kv = n_heads // group
    dv = vt.shape[2] // n_kv
    nparts = len(qs)
    assert nkv % 2 == 0 and nkv * tk == skv
    in_specs = [pl.BlockSpec((None, tq, HEAD_DIM), lambda bi, h, qi: (bi, qi, h)) for _ in qs]
    for k in ks:
        if k.shape[2] == n_kv * HEAD_DIM:
            in_specs.append(pl.BlockSpec((None, skv, HEAD_DIM), lambda bi, h, qi: (bi, 0, h // group)))
        else:
            in_specs.append(pl.BlockSpec((None, skv, HEAD_DIM), lambda bi, h, qi: (bi, 0, 0)))
    in_specs.append(pl.BlockSpec((None, nkv, dv, tk), lambda bi, h, qi: (bi, 0, h // group, 0)))
    scratch = []
    if nparts > 1:
        scratch.append(pltpu.VMEM((skv, nparts * HEAD_DIM), BF16))
    scratch += [pltpu.VMEM((nkv, dv + ONES_ROWS, tk), BF16), pltpu.VMEM((2, tk, tq), F32),
                pltpu.VMEM((1, tq), F32), pltpu.VMEM((dv + ONES_ROWS, tq), F32)]
    kern = functools.partial(_attn_t_kernel, nparts=nparts, tk=tk, nkv=nkv)
    return pl.pallas_call(
        kern,
        grid=(b, n_heads, s // tq),
        in_specs=in_specs,
        out_specs=pl.BlockSpec((None, tq, dv), lambda bi, h, qi: (bi, qi, h)),
        out_shape=jax.ShapeDtypeStruct((b, s, n_heads * dv), BF16),
        scratch_shapes=scratch,
        compiler_params=_params(),
        name="attention_t",
    )(*qs, *ks, vt)


def _mm_vt_kernel(*refs, scaled):
    a_ref, wt_ref, o_ref = refs[0], refs[1], refs[-1]
    out = lax.dot_general(wt_ref[...], a_ref[...], (((1,), (1,)), ((), ())), preferred_element_type=F32)
    if scaled:
        out = out * refs[2][...].T[:1, :]
    o_ref[...] = out.astype(o_ref.dtype)


def mm_vt(a, wt, tk, tn, r=None):
    m, k = a.shape
    cols = wt.shape[0]
    in_specs = [pl.BlockSpec((tk, k), lambda i, j: (i, 0)),
                pl.BlockSpec((tn, k), lambda i, j: (j, 0))]
    args = [a, wt]
    if r is not None:
        in_specs.append(_r_spec(tk))
        args.append(r)
    return pl.pallas_call(
        functools.partial(_mm_vt_kernel, scaled=r is not None),
        grid=(m // tk, cols // tn),
        in_specs=in_specs,
        out_specs=pl.BlockSpec((None, tn, tk), lambda i, j: (i, j, 0)),
        out_shape=jax.ShapeDtypeStruct((m // tk, cols, tk), BF16),
        compiler_params=_params(),
        name="mm_vt",
    )(*args)


def _deinterleave(w, width):
    lead = w.shape[:-1]
    n = w.shape[-1] // width
    w = w.reshape(lead + (n, width // 2, 2))
    w = jnp.swapaxes(w, -1, -2)
    return w.reshape(lead + (n * width,))


def _pad_groups(w, width, to):
    lead = w.shape[:-1]
    n = w.shape[-1] // width
    w = w.reshape(lead + (n, width))
    w = jnp.pad(w, [(0, 0)] * len(lead) + [(0, 0), (0, to - width)])
    return w.reshape(lead + (n * to,))


def _rope_angles(seq_len, rot_dim):
    rows = seq_len // GRID_W
    row, col = jnp.meshgrid(jnp.arange(rows), jnp.arange(GRID_W), indexing="ij")
    row = row.reshape(-1).astype(F32)
    col = col.reshape(-1).astype(F32)
    axis_dim = rot_dim // 2
    inv_freq = ROPE_THETA ** (-jnp.arange(0, axis_dim, 2, dtype=F32) / axis_dim)
    ang = jnp.concatenate([row[:, None] * inv_freq, col[:, None] * inv_freq], axis=-1)
    return jnp.cos(ang), jnp.sin(ang)


def _rope_tables(seq_len, rot_dim, width):
    cos, sin = _rope_angles(seq_len, rot_dim)
    half = rot_dim // 2
    z = jnp.zeros((seq_len, width - rot_dim), F32)
    zh = jnp.zeros_like(sin)
    c = jnp.concatenate([cos, cos, z], axis=-1)
    if rot_dim == width:
        return (half,), [c, jnp.concatenate([-sin, sin], axis=-1)]
    s_a = jnp.concatenate([-sin, zh, z], axis=-1)
    s_b = jnp.concatenate([zh, sin, z], axis=-1)
    return (width - half, half), [c, s_a, s_b]


def kernel(x, mem, ffn1_norm, ffn1_w_gu, ffn1_w_down, mix_norm, w_o, mem_norm, w_mem_kv, mem_q_norm, mem_k_norm, ffn2_norm, ffn2_w_gu, ffn2_w_down, a_w_in, a_q_a_norm, a_kv_a_norm, a_w_q_b, a_w_kv_b, a_q_nope_norm, a_q_pe_norm, a_k_nope_norm, a_k_pe_norm, b_w_in, b_q_norm, b_k_norm):
    b, s, d = x.shape
    n = b * s
    m_tok = mem.shape[1]
    depth = ffn1_norm.shape[0]
    hd = HEAD_DIM
    mem_w = w_mem_kv.shape[2] // 2
    mem_hd = mem_w // MEM_HEADS
    tok_w = w_o.shape[1] - mem_w
    n_heads = tok_w // hd
    q_lora = a_q_a_norm.shape[1]
    kv_lora = a_kv_a_norm.shape[1]
    n_kv = n_heads // B_GROUP

    tm = min(1024, s)
    seq_tiles = s // tm
    tq = min(1024, s)
    tk = min(512, s // 2)
    tm_mem = min(512, b * m_tok)
    tm_norm = min(256, s)

    def tile(width, pref):
        t = min(pref, width)
        while width % t:
            t -= hd
        return t

    rope_a = _rope_tables(s, A_ROPE, hd)
    rope_b = _rope_tables(s, hd, hd)
    scale_a = float((hd + A_ROPE) ** -0.5) * LOG2E
    scale_b = float(hd ** -0.5) * LOG2E
    scale_m = float(mem_hd ** -0.5)

    xf = x.reshape(n, d)
    memf = mem.reshape(b * m_tok, d)
    w_gu1, w_gu2 = ffn1_w_gu.astype(BF16), ffn2_w_gu.astype(BF16)
    w_dn1, w_dn2 = ffn1_w_down.astype(BF16), ffn2_w_down.astype(BF16)
    w_ob = w_o.astype(BF16)
    d_ff = w_gu1.shape[2] // 2
    r3 = lambda t: t.reshape(b, s, t.shape[-1])

    def produce(a_list, w, scale, g_next):
        x_new, xg, ss = mm_residual(a_list, w, i, xf, scale, g_next, tm, tile(d, 512))
        return x_new, xg, row_scale(ss, d, tm)

    xg, r = scale_prep(xf, ffn1_norm[0], tm_norm)
    for i in range(depth):
        act = mm_swiglu(xg, r, w_gu1, i, tm, tile(d_ff, 512))
        xf, xg, r = produce([act], w_dn1, 0.5, mix_norm[i])
        proj = functools.partial(mm_headnorm, r=r)
        j = i // N_MIXERS
        tq_a, tk_a = ((2048, 1024), (2048, 1024), (1024, 1024), (1024, 2048))[i % 4] if s >= 8192 else (tq, tk)
        if i % N_MIXERS == 0:
            w_in = a_w_in[j]
            o1, o2, o3 = q_lora, q_lora + kv_lora, q_lora + kv_lora + A_ROPE
            w_cq = w_in[:, :o1].astype(BF16)
            w_ckv = w_in[:, o1:o2].astype(BF16)
            w_kpe = _pad_groups(_deinterleave(w_in[:, o2:o3], A_ROPE), A_ROPE, hd).astype(BF16)
            w_qm = w_in[:, o3:].astype(BF16)
            wq = a_w_q_b[j].reshape(q_lora, n_heads, hd + A_ROPE)
            w_qn = wq[:, :, :hd].reshape(q_lora, n_heads * hd).astype(BF16)
            w_qp = _pad_groups(_deinterleave(wq[:, :, hd:].reshape(q_lora, n_heads * A_ROPE), A_ROPE),
                               A_ROPE, hd).astype(BF16)
            wkv = a_w_kv_b[j].reshape(kv_lora, n_heads, 2 * hd)
            w_kn = wkv[:, :, :hd].reshape(kv_lora, n_heads * hd).astype(BF16)
            w_v = wkv[:, :, hd:].reshape(kv_lora, n_heads * hd).astype(BF16)
            g_qpe = _pad_groups(_deinterleave(a_q_pe_norm[j], A_ROPE), A_ROPE, hd) * scale_a
            g_kpe = _pad_groups(_deinterleave(a_k_pe_norm[j], A_ROPE), A_ROPE, hd)

            cqn = proj(xg, w_cq, a_q_a_norm[j], q_lora, q_lora, tm, q_lora)
            ckvn = proj(xg, w_ckv, a_kv_a_norm[j], kv_lora, kv_lora, tm, kv_lora)
            kpe = proj(xg, w_kpe, g_kpe, hd, A_ROPE, tm, hd, rope=rope_a, seq_tiles=seq_tiles)
            mq = proj(xg, w_qm, mem_q_norm[i] * scale_m, mem_hd, mem_hd, tm, tile(mem_w, 1024))
            tw = tile(n_heads * hd, 1024)
            qn = mm_headnorm(cqn, w_qn, a_q_nope_norm[j] * scale_a, hd, hd, tm, tw)
            qp = mm_headnorm(cqn, w_qp, g_qpe, hd, A_ROPE, tm, tw, rope=rope_a, seq_tiles=seq_tiles)
            kn = mm_headnorm(ckvn, w_kn, a_k_nope_norm[j], hd, hd, tm, tw)
            vt = mm_vt(ckvn, w_v.T, tk_a, tw).reshape(b, s // tk_a, n_heads * hd, tk_a)
            tok = attention_t([r3(qn), r3(qp)], [r3(kn), r3(kpe)], vt, n_heads, 1, tq_a)
        else:
            w_in = b_w_in[j]
            qw, kw = n_heads * hd, n_kv * hd
            w_q = _deinterleave(w_in[:, :qw], hd).astype(BF16)
            w_k = _deinterleave(w_in[:, qw:qw + kw], hd).astype(BF16)
            w_v = w_in[:, qw + kw:qw + 2 * kw].astype(BF16)
            w_qm = w_in[:, qw + 2 * kw:].astype(BF16)
            g_q = _deinterleave(b_q_norm[j], hd) * scale_b
            g_k = _deinterleave(b_k_norm[j], hd)
            q = proj(xg, w_q, g_q, hd, hd, tm, tile(qw, 1024), rope=rope_b, seq_tiles=seq_tiles)
            k = proj(xg, w_k, g_k, hd, hd, tm, tile(kw, 1024), rope=rope_b, seq_tiles=seq_tiles)
            vt = mm_vt(xg, w_v.T, tk_a, tile(kw, 1024), r=r).reshape(b, s // tk_a, kw, tk_a)
            mq = proj(xg, w_qm, mem_q_norm[i] * scale_m, mem_hd, mem_hd, tm, tile(mem_w, 1024))
            tok = attention_t([r3(q)], [r3(k)], vt, n_heads, B_GROUP, tq_a)

        memn = rmsnorm(memf, mem_norm[i], tm_mem)
        w_mkv = w_mem_kv[i].astype(BF16)
        mk = mm_headnorm(memn, w_mkv[:, :mem_w], mem_k_norm[i], mem_hd, mem_hd, tm_mem, tile(mem_w, 1024))
        mv = mm_cast(memn, w_mkv[:, mem_w:], tm_mem, tile(mem_w, 1024))
        mo = attention([mq.reshape(b, s, mem_w)], [mk.reshape(b, m_tok, mem_w)], mv.reshape(b, m_tok, mem_w),
                       MEM_HEADS, 1, mem_hd, mem_hd, tq, m_tok)
        xf, xg, r = produce([tok.reshape(n, tok_w), mo.reshape(n, mem_w)], w_ob, 1.0, ffn2_norm[i])
        act = mm_swiglu(xg, r, w_gu2, i, tm, tile(d_ff, 512))
        if i + 1 < depth:
            xf, xg, r = produce([act], w_dn2, 0.5, ffn1_norm[i + 1])
        else:
            xf = mm_residual([act], w_dn2, i, xf, 0.5, None, tm, tile(d, 512))
    return xf.reshape(b, s, d)
```

```python
import functools

import jax
import jax.numpy as jnp
from jax import lax
from jax.experimental import pallas as pl
from jax.experimental.pallas import tpu as pltpu

EPS = 1e-6
GRID_W = 64
ROPE_THETA = 10000.0
LANES = 128
HEAD_DIM = LANES
A_ROPE = 64
MEM_HEADS = 4
LOG2E = 1.4426950408889634
N_MIXERS = 2
B_GROUP = 4
ONES_ROWS = 16
MAX_STATIC_CHUNKS = 8
V7X_VMEM_BYTES = 64 * 1024 * 1024
VMEM_LIMIT = V7X_VMEM_BYTES * 7 // 8

F32 = jnp.float32
BF16 = jnp.bfloat16


def _params():
    return pltpu.CompilerParams(vmem_limit_bytes=VMEM_LIMIT)


def _rmsnorm_kernel(x_ref, g_ref, o_ref):
    x = x_ref[...]
    ms = jnp.mean(x * x, axis=-1, keepdims=True)
    o_ref[...] = (x * lax.rsqrt(ms + EPS) * g_ref[...]).astype(o_ref.dtype)


def rmsnorm(x, g, tm):
    m, d = x.shape
    return pl.pallas_call(
        _rmsnorm_kernel,
        grid=(m // tm,),
        in_specs=[pl.BlockSpec((tm, d), lambda i: (i, 0)),
                  pl.BlockSpec((1, d), lambda i: (0, 0))],
        out_specs=pl.BlockSpec((tm, d), lambda i: (i, 0)),
        out_shape=jax.ShapeDtypeStruct((m, d), BF16),
        compiler_params=_params(),
        name="rmsnorm",
    )(x, g.reshape(1, d))


def _row_scale_kernel(ss_ref, r_ref, *, inv_d):
    ss = ss_ref[...]
    tot = ss[:, :LANES]
    for c in range(1, ss.shape[1] // LANES):
        tot = tot + ss[:, c * LANES:(c + 1) * LANES]
    r_ref[...] = lax.rsqrt(tot * inv_d + EPS)


def row_scale(ss, d_norm, tm):
    m, w = ss.shape
    return pl.pallas_call(
        functools.partial(_row_scale_kernel, inv_d=1.0 / d_norm),
        grid=(m // tm,),
        in_specs=[pl.BlockSpec((tm, w), lambda i: (i, 0))],
        out_specs=pl.BlockSpec((tm, LANES), lambda i: (i, 0)),
        out_shape=jax.ShapeDtypeStruct((m, LANES), F32),
        compiler_params=_params(),
        name="row_scale",
    )(ss)


def _widen(r, width):
    return r if width == LANES else jnp.tile(r, (1, width // LANES))


def _scale_prep_kernel(x_ref, g_ref, xg_ref, r_ref):
    x = x_ref[...]
    xg_ref[...] = (x * g_ref[...]).astype(BF16)
    r = lax.rsqrt(jnp.mean(x * x, axis=-1, keepdims=True) + EPS)
    r_ref[...] = jnp.broadcast_to(r, r_ref.shape)


def scale_prep(x, g, tm):
    m, d = x.shape
    return pl.pallas_call(
        _scale_prep_kernel,
        grid=(m // tm,),
        in_specs=[pl.BlockSpec((tm, d), lambda i: (i, 0)),
                  pl.BlockSpec((1, d), lambda i: (0, 0))],
        out_specs=[pl.BlockSpec((tm, d), lambda i: (i, 0)),
                   pl.BlockSpec((tm, LANES), lambda i: (i, 0))],
        out_shape=[jax.ShapeDtypeStruct((m, d), BF16), jax.ShapeDtypeStruct((m, LANES), F32)],
        compiler_params=_params(),
        name="scale_prep",
    )(x, g.reshape(1, d))


def _mm_cast_kernel(a_ref, b_ref, o_ref):
    acc = jnp.dot(a_ref[...], b_ref[...], preferred_element_type=F32)
    o_ref[...] = acc.astype(o_ref.dtype)


def _mm_swiglu_kernel(a_ref, r_ref, bg_ref, bu_ref, o_ref):
    a = a_ref[...]
    r = _widen(r_ref[...], o_ref.shape[1])
    g = jnp.dot(a, bg_ref[...], preferred_element_type=F32) * r
    u = jnp.dot(a, bu_ref[...], preferred_element_type=F32) * r
    o_ref[...] = (g * (1.0 / (1.0 + jnp.exp(-g))) * u).astype(o_ref.dtype)


def _mm_residual_kernel(*refs, n_a, scale, emit_next):
    a_refs = refs[:n_a]
    b_refs = refs[n_a:2 * n_a]
    r_ref = refs[2 * n_a]
    acc = jnp.dot(a_refs[0][...], b_refs[0][...], preferred_element_type=F32)
    for a_ref, b_ref in zip(a_refs[1:], b_refs[1:]):
        acc = acc + jnp.dot(a_ref[...], b_ref[...], preferred_element_type=F32)
    x = r_ref[...] + (acc if scale == 1.0 else scale * acc)
    if emit_next:
        g_ref = refs[2 * n_a + 1]
        o_ref, xg_ref, ss_ref = refs[-3:]
        xg_ref[...] = (x * g_ref[...]).astype(BF16)
        ss_ref[...] = jnp.broadcast_to(jnp.sum(x * x, axis=-1, keepdims=True), ss_ref.shape)
    else:
        o_ref = refs[-1]
    o_ref[...] = x


def _mm_headnorm_kernel(*refs, cw, inv_count, shifts, scaled):
    a_ref, b_ref = refs[0], refs[1]
    nxt = 2
    acc = jnp.dot(a_ref[...], b_ref[...], preferred_element_type=F32)
    if scaled:
        acc = acc * _widen(refs[2][...], acc.shape[1])
        nxt = 3
    g = refs[nxt][...]
    tabs = refs[nxt + 1:-1]
    o_ref = refs[-1]
    for c in range(acc.shape[1] // cw):
        x = acc[:, c * cw:(c + 1) * cw]
        ms = jnp.sum(x * x, axis=-1, keepdims=True) * inv_count
        y = x * lax.rsqrt(ms + EPS) * g
        if shifts:
            out = y * tabs[0][...]
            for k, sh in enumerate(shifts):
                out = out + pltpu.roll(y, sh, 1) * tabs[1 + k][...]
            y = out
        o_ref[:, c * cw:(c + 1) * cw] = y.astype(o_ref.dtype)


def _r_spec(tm):
    return pl.BlockSpec((tm, LANES), lambda i, j: (i, 0))


def mm_cast(a, b, tm, tn, out_dtype=BF16):
    m, k = a.shape
    n_out = b.shape[1]
    return pl.pallas_call(
        _mm_cast_kernel,
        grid=(m // tm, n_out // tn),
        in_specs=[pl.BlockSpec((tm, k), lambda i, j: (i, 0)),
                  pl.BlockSpec((k, tn), lambda i, j: (0, j))],
        out_specs=pl.BlockSpec((tm, tn), lambda i, j: (i, j)),
        out_shape=jax.ShapeDtypeStruct((m, n_out), out_dtype),
        compiler_params=_params(),
        name="mm_cast",
    )(a, b)


def mm_swiglu(xg, r, w_gu, layer, tm, tn):
    m, k = xg.shape
    d_ff = w_gu.shape[2] // 2
    off = d_ff // tn
    return pl.pallas_call(
        _mm_swiglu_kernel,
        grid=(m // tm, d_ff // tn),
        in_specs=[pl.BlockSpec((tm, k), lambda i, j: (i, 0)),
                  _r_spec(tm),
                  pl.BlockSpec((None, k, tn), lambda i, j: (layer, 0, j)),
                  pl.BlockSpec((None, k, tn), lambda i, j: (layer, 0, j + off))],
        out_specs=pl.BlockSpec((tm, tn), lambda i, j: (i, j)),
        out_shape=jax.ShapeDtypeStruct((m, d_ff), BF16),
        compiler_params=_params(),
        name="mm_swiglu",
    )(xg, r, w_gu, w_gu)


def mm_residual(a_list, w, layer, res, scale, g_next, tm, tn):
    m = a_list[0].shape[0]
    n_out = w.shape[2]
    in_specs = [pl.BlockSpec((tm, a.shape[1]), lambda i, j: (i, 0)) for a in a_list]
    row = 0
    for a in a_list:
        ka = a.shape[1]
        assert row % ka == 0
        in_specs.append(pl.BlockSpec((None, ka, tn), functools.partial(lambda i, j, rb: (layer, rb, j), rb=row // ka)))
        row += ka
    assert row == w.shape[1]
    tile_spec = pl.BlockSpec((tm, tn), lambda i, j: (i, j))
    in_specs.append(tile_spec)
    args = list(a_list) + [w] * len(a_list) + [res]
    emit_next = g_next is not None
    if emit_next:
        in_specs.append(pl.BlockSpec((1, tn), lambda i, j: (0, j)))
        args.append(g_next.reshape(1, n_out))
        out_specs = [tile_spec, tile_spec, pl.BlockSpec((tm, LANES), lambda i, j: (i, j))]
        out_shape = [jax.ShapeDtypeStruct((m, n_out), F32), jax.ShapeDtypeStruct((m, n_out), BF16),
                     jax.ShapeDtypeStruct((m, (n_out // tn) * LANES), F32)]
    else:
        out_specs = tile_spec
        out_shape = jax.ShapeDtypeStruct((m, n_out), F32)
    return pl.pallas_call(
        functools.partial(_mm_residual_kernel, n_a=len(a_list), scale=scale, emit_next=emit_next),
        grid=(m // tm, n_out // tn),
        in_specs=in_specs,
        out_specs=out_specs,
        out_shape=out_shape,
        compiler_params=_params(),
        name="mm_residual",
    )(*args)


def mm_headnorm(a, b, g, cw, count, tm, tn, rope=None, seq_tiles=1, r=None):
    m, k = a.shape
    n_out = b.shape[1]
    in_specs = [pl.BlockSpec((tm, k), lambda i, j: (i, 0)),
                pl.BlockSpec((k, tn), lambda i, j: (0, j))]
    args = [a, b]
    if r is not None:
        in_specs.append(_r_spec(tm))
        args.append(r)
    in_specs.append(pl.BlockSpec((1, cw), lambda i, j: (0, 0)))
    args.append(g.reshape(1, cw))
    shifts = ()
    if rope is not None:
        shifts, tabs = rope
        for t in tabs:
            in_specs.append(pl.BlockSpec((tm, cw), lambda i, j: (i % seq_tiles, 0)))
            args.append(t)
    kern = functools.partial(_mm_headnorm_kernel, cw=cw, inv_count=1.0 / count, shifts=tuple(shifts),
                             scaled=r is not None)
    return pl.pallas_call(
        kern,
        grid=(m // tm, n_out // tn),
        in_specs=in_specs,
        out_specs=pl.BlockSpec((tm, tn), lambda i, j: (i, j)),
        out_shape=jax.ShapeDtypeStruct((m, n_out), BF16),
        compiler_params=_params(),
        name="mm_headnorm",
    )(*args)


def _attn_kernel(*refs, nparts, tk, nkv):
    q_refs = refs[:nparts]
    k_refs = refs[nparts:2 * nparts]
    v_ref = refs[2 * nparts]
    o_ref = refs[2 * nparts + 1]
    if nparts > 1:
        q = jnp.concatenate([r[...] for r in q_refs], axis=-1)
    else:
        q = q_refs[0][...]
    tq = q.shape[0]
    dv = v_ref.shape[-1]

    def body(t, carry):
        m, l, acc = carry
        off = pl.multiple_of(t * tk, tk)
        if nparts > 1:
            k = jnp.concatenate([r[pl.ds(off, tk), :] for r in k_refs], axis=-1)
        else:
            k = k_refs[0][pl.ds(off, tk), :]
        s = lax.dot_general(q, k, (((1,), (1,)), ((), ())), preferred_element_type=F32)
        m_new = jnp.maximum(m, jnp.max(s, axis=-1, keepdims=True))
        p = jnp.exp(s - m_new)
        alpha = jnp.exp(m - m_new)
        l = alpha * l + jnp.sum(p, axis=-1, keepdims=True)
        acc = alpha * acc + jnp.dot(p.astype(BF16), v_ref[pl.ds(off, tk), :], preferred_element_type=F32)
        return m_new, l, acc

    init = (jnp.full((tq, 1), -jnp.inf, F32), jnp.zeros((tq, 1), F32), jnp.zeros((tq, dv), F32))
    if nkv == 1:
        m, l, acc = body(0, init)
    else:
        m, l, acc = lax.fori_loop(0, nkv, body, init)
    o_ref[...] = (acc / l).astype(o_ref.dtype)


def attention(qs, ks, v, n_heads, group, dq, dv, tq, tk):
    b, s, _ = qs[0].shape
    skv = v.shape[1]
    nparts = len(qs)
    in_specs = [pl.BlockSpec((None, tq, dq), lambda bi, h, qi: (bi, qi, h)) for _ in qs]
    for k in ks:
        if k.shape[2] == dq:
            in_specs.append(pl.BlockSpec((None, skv, dq), lambda bi, h, qi: (bi, 0, 0)))
        else:
            in_specs.append(pl.BlockSpec((None, skv, dq), lambda bi, h, qi: (bi, 0, h // group)))
    in_specs.append(pl.BlockSpec((None, skv, dv), lambda bi, h, qi: (bi, 0, h // group)))
    kern = functools.partial(_attn_kernel, nparts=nparts, tk=tk, nkv=skv // tk)
    return pl.pallas_call(
        kern,
        grid=(b, n_heads, s // tq),
        in_specs=in_specs,
        out_specs=pl.BlockSpec((None, tq, dv), lambda bi, h, qi: (bi, qi, h)),
        out_shape=jax.ShapeDtypeStruct((b, s, n_heads * dv), BF16),
        compiler_params=_params(),
        name="attention",
    )(*qs, *ks, v)


def _attn_t_kernel(*refs, nparts, tk, nkv):
    q_refs = refs[:nparts]
    k_refs = refs[nparts:2 * nparts]
    vt_ref = refs[2 * nparts]
    o_ref = refs[2 * nparts + 1]
    scratch = refs[2 * nparts + 2:]
    dv = vt_ref.shape[1]
    first_q_tile = pl.program_id(2) == 0
    if nparts > 1:
        kcat_ref, vx_ref, s_ref, m_ref, acc_ref = scratch
        q = jnp.concatenate([r[...] for r in q_refs], axis=-1)

        @pl.when(first_q_tile)
        def _():
            for i, r in enumerate(k_refs):
                kcat_ref[:, i * HEAD_DIM:(i + 1) * HEAD_DIM] = r[...]
        k_ref = kcat_ref
    else:
        vx_ref, s_ref, m_ref, acc_ref = scratch
        q = q_refs[0][...]
        k_ref = k_refs[0]

    @pl.when(first_q_tile)
    def _():
        vx_ref[:, :dv, :] = vt_ref[...]
        vx_ref[:, dv:, :] = jnp.ones((nkv, ONES_ROWS, tk), BF16)

    qt = q.astype(F32).T.astype(BF16)

    def qk(c, slot):
        off = pl.multiple_of(c * tk, tk)
        s_ref[slot] = jnp.dot(k_ref[pl.ds(off, tk), :], qt, preferred_element_type=F32)

    def softmax_pv(c, slot):
        s = s_ref[slot]
        m_old = m_ref[...]
        m_new = jnp.maximum(m_old, jnp.max(s, axis=0, keepdims=True))
        alpha = jnp.exp2(m_old - m_new)
        p = jnp.exp2(s - m_new)
        m_ref[...] = m_new
        acc_ref[...] = alpha * acc_ref[...] + jnp.dot(vx_ref[c], p.astype(BF16), preferred_element_type=F32)

    m_ref[...] = jnp.full(m_ref.shape, -jnp.inf, F32)
    acc_ref[...] = jnp.zeros(acc_ref.shape, F32)
    qk(0, 0)

    def pair(i, carry):
        c = 2 * i
        qk(c + 1, 1)
        softmax_pv(c, 0)
        qk(c + 2, 0)
        softmax_pv(c + 1, 1)
        return carry

    if nkv <= MAX_STATIC_CHUNKS:
        for i in range(nkv // 2 - 1):
            pair(i, 0)
    else:
        lax.fori_loop(0, nkv // 2 - 1, pair, 0)
    qk(nkv - 1, 1)
    softmax_pv(nkv - 2, 0)
    softmax_pv(nkv - 1, 1)
    o_ref[...] = (acc_ref[:dv, :] * (1.0 / acc_ref[dv:dv + 1, :])).T.astype(o_ref.dtype)


def attention_t(qs, ks, vt, n_heads, group, tq):
    b, s, _ = qs[0].shape
    skv = ks[0].shape[1]
    nkv, tk = vt.shape[1], vt.shape[3]
    n_kv = n_heads // group
    dv = vt.shape[2] // n_kv
    nparts = len(qs)
    assert nkv % 2 == 0 and nkv * tk == skv
    in_specs = [pl.BlockSpec((None, tq, HEAD_DIM), lambda bi, h, qi: (bi, qi, h)) for _ in qs]
    for k in ks:
        if k.shape[2] == n_kv * HEAD_DIM:
            in_specs.append(pl.BlockSpec((None, skv, HEAD_DIM), lambda bi, h, qi: (bi, 0, h // group)))
        else:
            in_specs.append(pl.BlockSpec((None, skv, HEAD_DIM), lambda bi, h, qi: (bi, 0, 0)))
    in_specs.append(pl.BlockSpec((None, nkv, dv, tk), lambda bi, h, qi: (bi, 0, h // group, 0)))
    scratch = []
    if nparts > 1:
        scratch.append(pltpu.VMEM((skv, nparts * HEAD_DIM), BF16))
    scratch += [pltpu.VMEM((nkv, dv + ONES_ROWS, tk), BF16), pltpu.VMEM((2, tk, tq), F32),
                pltpu.VMEM((1, tq), F32), pltpu.VMEM((dv + ONES_ROWS, tq), F32)]
    kern = functools.partial(_attn_t_kernel, nparts=nparts, tk=tk, nkv=nkv)
    return pl.pallas_call(
        kern,
        grid=(b, n_heads, s // tq),
        in_specs=in_specs,
        out_specs=pl.BlockSpec((None, tq, dv), lambda bi, h, qi: (bi, qi, h)),
        out_shape=jax.ShapeDtypeStruct((b, s, n_heads * dv), BF16),
        scratch_shapes=scratch,
        compiler_params=_params(),
        name="attention_t",
    )(*qs, *ks, vt)


def _mm_vt_kernel(*refs, scaled):
    a_ref, wt_ref, o_ref = refs[0], refs[1], refs[-1]
    out = lax.dot_general(wt_ref[...], a_ref[...], (((1,), (1,)), ((), ())), preferred_element_type=F32)
    if scaled:
        out = out * refs[2][...].T[:1, :]
    o_ref[...] = out.astype(o_ref.dtype)


def mm_vt(a, wt, tk, tn, r=None):
    m, k = a.shape
    cols = wt.shape[0]
    tr = min(tk, 512)
    per = tk // tr
    in_specs = [pl.BlockSpec((tr, k), lambda i, j: (i, 0)),
                pl.BlockSpec((tn, k), lambda i, j: (j, 0))]
    args = [a, wt]
    if r is not None:
        in_specs.append(_r_spec(tr))
        args.append(r)
    return pl.pallas_call(
        functools.partial(_mm_vt_kernel, scaled=r is not None),
        grid=(m // tr, cols // tn),
        in_specs=in_specs,
        out_specs=pl.BlockSpec((None, tn, tr), lambda i, j: (i // per, j, i % per)),
        out_shape=jax.ShapeDtypeStruct((m // tk, cols, tk), BF16),
        compiler_params=_params(),
        name="mm_vt",
    )(*args)


def _deinterleave(w, width):
    lead = w.shape[:-1]
    n = w.shape[-1] // width
    w = w.reshape(lead + (n, width // 2, 2))
    w = jnp.swapaxes(w, -1, -2)
    return w.reshape(lead + (n * width,))


def _pad_groups(w, width, to):
    lead = w.shape[:-1]
    n = w.shape[-1] // width
    w = w.reshape(lead + (n, width))
    w = jnp.pad(w, [(0, 0)] * len(lead) + [(0, 0), (0, to - width)])
    return w.reshape(lead + (n * to,))


def _rope_angles(seq_len, rot_dim):
    rows = seq_len // GRID_W
    row, col = jnp.meshgrid(jnp.arange(rows), jnp.arange(GRID_W), indexing="ij")
    row = row.reshape(-1).astype(F32)
    col = col.reshape(-1).astype(F32)
    axis_dim = rot_dim // 2
    inv_freq = ROPE_THETA ** (-jnp.arange(0, axis_dim, 2, dtype=F32) / axis_dim)
    ang = jnp.concatenate([row[:, None] * inv_freq, col[:, None] * inv_freq], axis=-1)
    return jnp.cos(ang), jnp.sin(ang)


def _rope_tables(seq_len, rot_dim, width):
    cos, sin = _rope_angles(seq_len, rot_dim)
    half = rot_dim // 2
    z = jnp.zeros((seq_len, width - rot_dim), F32)
    zh = jnp.zeros_like(sin)
    c = jnp.concatenate([cos, cos, z], axis=-1)
    if rot_dim == width:
        return (half,), [c, jnp.concatenate([-sin, sin], axis=-1)]
    s_a = jnp.concatenate([-sin, zh, z], axis=-1)
    s_b = jnp.concatenate([zh, sin, z], axis=-1)
    return (width - half, half), [c, s_a, s_b]


def kernel(x, mem, ffn1_norm, ffn1_w_gu, ffn1_w_down, mix_norm, w_o, mem_norm, w_mem_kv, mem_q_norm, mem_k_norm, ffn2_norm, ffn2_w_gu, ffn2_w_down, a_w_in, a_q_a_norm, a_kv_a_norm, a_w_q_b, a_w_kv_b, a_q_nope_norm, a_q_pe_norm, a_k_nope_norm, a_k_pe_norm, b_w_in, b_q_norm, b_k_norm):
    b, s, d = x.shape
    n = b * s
    m_tok = mem.shape[1]
    depth = ffn1_norm.shape[0]
    hd = HEAD_DIM
    mem_w = w_mem_kv.shape[2] // 2
    mem_hd = mem_w // MEM_HEADS
    tok_w = w_o.shape[1] - mem_w
    n_heads = tok_w // hd
    q_lora = a_q_a_norm.shape[1]
    kv_lora = a_kv_a_norm.shape[1]
    n_kv = n_heads // B_GROUP

    tm = min(1024, s)
    seq_tiles = s // tm
    tq = min(1024, s)
    tk = min(512, s // 2)
    tm_mem = min(512, b * m_tok)
    tm_norm = min(256, s)

    def tile(width, pref):
        t = min(pref, width)
        while width % t:
            t -= hd
        return t

    rope_a = _rope_tables(s, A_ROPE, hd)
    rope_b = _rope_tables(s, hd, hd)
    scale_a = float((hd + A_ROPE) ** -0.5) * LOG2E
    scale_b = float(hd ** -0.5) * LOG2E
    scale_m = float(mem_hd ** -0.5)

    xf = x.reshape(n, d)
    memf = mem.reshape(b * m_tok, d)
    w_gu1, w_gu2 = ffn1_w_gu.astype(BF16), ffn2_w_gu.astype(BF16)
    w_dn1, w_dn2 = ffn1_w_down.astype(BF16), ffn2_w_down.astype(BF16)
    w_ob = w_o.astype(BF16)
    d_ff = w_gu1.shape[2] // 2
    r3 = lambda t: t.reshape(b, s, t.shape[-1])

    def produce(a_list, w, scale, g_next):
        x_new, xg, ss = mm_residual(a_list, w, i, xf, scale, g_next, tm, tile(d, 512))
        return x_new, xg, row_scale(ss, d, tm)

    xg, r = scale_prep(xf, ffn1_norm[0], tm_norm)
    for i in range(depth):
        act = mm_swiglu(xg, r, w_gu1, i, tm, tile(d_ff, 512))
        xf, xg, r = produce([act], w_dn1, 0.5, mix_norm[i])
        proj = functools.partial(mm_headnorm, r=r)
        j = i // N_MIXERS
        tq_a, tk_a = ((1024, 2048), (1024, 1024), (1024, 1024), (512, 4096))[i % 4] if s >= 8192 else (tq, tk)
        if i % N_MIXERS == 0:
            w_in = a_w_in[j]
            o1, o2, o3 = q_lora, q_lora + kv_lora, q_lora + kv_lora + A_ROPE
            w_cq = w_in[:, :o1].astype(BF16)
            w_ckv = w_in[:, o1:o2].astype(BF16)
            w_kpe = _pad_groups(_deinterleave(w_in[:, o2:o3], A_ROPE), A_ROPE, hd).astype(BF16)
            w_qm = w_in[:, o3:].astype(BF16)
            wq = a_w_q_b[j].reshape(q_lora, n_heads, hd + A_ROPE)
            w_qn = wq[:, :, :hd].reshape(q_lora, n_heads * hd).astype(BF16)
            w_qp = _pad_groups(_deinterleave(wq[:, :, hd:].reshape(q_lora, n_heads * A_ROPE), A_ROPE),
                               A_ROPE, hd).astype(BF16)
            wkv = a_w_kv_b[j].reshape(kv_lora, n_heads, 2 * hd)
            w_kn = wkv[:, :, :hd].reshape(kv_lora, n_heads * hd).astype(BF16)
            w_v = wkv[:, :, hd:].reshape(kv_lora, n_heads * hd).astype(BF16)
            g_qpe = _pad_groups(_deinterleave(a_q_pe_norm[j], A_ROPE), A_ROPE, hd) * scale_a
            g_kpe = _pad_groups(_deinterleave(a_k_pe_norm[j], A_ROPE), A_ROPE, hd)

            cqn = proj(xg, w_cq, a_q_a_norm[j], q_lora, q_lora, tm, q_lora)
            ckvn = proj(xg, w_ckv, a_kv_a_norm[j], kv_lora, kv_lora, tm, kv_lora)
            kpe = proj(xg, w_kpe, g_kpe, hd, A_ROPE, tm, hd, rope=rope_a, seq_tiles=seq_tiles)
            mq = proj(xg, w_qm, mem_q_norm[i] * scale_m, mem_hd, mem_hd, tm, tile(mem_w, 1024))
            tw = tile(n_heads * hd, 1024)
            qn = mm_headnorm(cqn, w_qn, a_q_nope_norm[j] * scale_a, hd, hd, tm, tw)
            qp = mm_headnorm(cqn, w_qp, g_qpe, hd, A_ROPE, tm, tw, rope=rope_a, seq_tiles=seq_tiles)
            kn = mm_headnorm(ckvn, w_kn, a_k_nope_norm[j], hd, hd, tm, tw)
            vt = mm_vt(ckvn, w_v.T, tk_a, tw).reshape(b, s // tk_a, n_heads * hd, tk_a)
            tok = attention_t([r3(qn), r3(qp)], [r3(kn), r3(kpe)], vt, n_heads, 1, tq_a)
        else:
            w_in = b_w_in[j]
            qw, kw = n_heads * hd, n_kv * hd
            w_q = _deinterleave(w_in[:, :qw], hd).astype(BF16)
            w_k = _deinterleave(w_in[:, qw:qw + kw], hd).astype(BF16)
            w_v = w_in[:, qw + kw:qw + 2 * kw].astype(BF16)
            w_qm = w_in[:, qw + 2 * kw:].astype(BF16)
            g_q = _deinterleave(b_q_norm[j], hd) * scale_b
            g_k = _deinterleave(b_k_norm[j], hd)
            q = proj(xg, w_q, g_q, hd, hd, tm, tile(qw, 1024), rope=rope_b, seq_tiles=seq_tiles)
            k = proj(xg, w_k, g_k, hd, hd, tm, tile(kw, 1024), rope=rope_b, seq_tiles=seq_tiles)
            vt = mm_vt(xg, w_v.T, tk_a, tile(kw, 1024), r=r).reshape(b, s // tk_a, kw, tk_a)
            mq = proj(xg, w_qm, mem_q_norm[i] * scale_m, mem_hd, mem_hd, tm, tile(mem_w, 1024))
            tok = attention_t([r3(q)], [r3(k)], vt, n_heads, B_GROUP, tq_a)

        memn = rmsnorm(memf, mem_norm[i], tm_mem)
        w_mkv = w_mem_kv[i].astype(BF16)
        mk = mm_headnorm(memn, w_mkv[:, :mem_w], mem_k_norm[i], mem_hd, mem_hd, tm_mem, tile(mem_w, 1024))
        mv = mm_cast(memn, w_mkv[:, mem_w:], tm_mem, tile(mem_w, 1024))
        mo = attention([mq.reshape(b, s, mem_w)], [mk.reshape(b, m_tok, mem_w)], mv.reshape(b, m_tok, mem_w),
                       MEM_HEADS, 1, mem_hd, mem_hd, tq, m_tok)
        xf, xg, r = produce([tok.reshape(n, tok_w), mo.reshape(n, mem_w)], w_ob, 1.0, ffn2_norm[i])
        act = mm_swiglu(xg, r, w_gu2, i, tm, tile(d_ff, 512))
        if i + 1 < depth:
            xf, xg, r = produce([act], w_dn2, 0.5, ffn1_norm[i + 1])
        else:
            xf = mm_residual([act], w_dn2, i, xf, 0.5, None, tm, tile(d, 512))
    return xf.reshape(b, s, d)
```

```python
import functools

import jax
import jax.numpy as jnp
from jax import lax
from jax.experimental import pallas as pl
from jax.experimental.pallas import tpu as pltpu

EPS = 1e-6
GRID_W = 64
ROPE_THETA = 10000.0
LANES = 128
HEAD_DIM = LANES
A_ROPE = 64
MEM_HEADS = 4
LOG2E = 1.4426950408889634
N_MIXERS = 2
B_GROUP = 4
ONES_ROWS = 16
MAX_STATIC_UNITS = 8
V7X_VMEM_BYTES = 64 * 1024 * 1024
VMEM_LIMIT = V7X_VMEM_BYTES * 7 // 8

F32 = jnp.float32
BF16 = jnp.bfloat16


def _params():
    return pltpu.CompilerParams(vmem_limit_bytes=VMEM_LIMIT)


def _rmsnorm_kernel(x_ref, g_ref, o_ref):
    x = x_ref[...]
    ms = jnp.mean(x * x, axis=-1, keepdims=True)
    o_ref[...] = (x * lax.rsqrt(ms + EPS) * g_ref[...]).astype(o_ref.dtype)


def rmsnorm(x, g, tm):
    m, d = x.shape
    return pl.pallas_call(
        _rmsnorm_kernel,
        grid=(m // tm,),
        in_specs=[pl.BlockSpec((tm, d), lambda i: (i, 0)),
                  pl.BlockSpec((1, d), lambda i: (0, 0))],
        out_specs=pl.BlockSpec((tm, d), lambda i: (i, 0)),
        out_shape=jax.ShapeDtypeStruct((m, d), BF16),
        compiler_params=_params(),
        name="rmsnorm",
    )(x, g.reshape(1, d))


def _row_scale_kernel(ss_ref, r_ref, *, inv_d):
    ss = ss_ref[...]
    tot = ss[:, :LANES]
    for c in range(1, ss.shape[1] // LANES):
        tot = tot + ss[:, c * LANES:(c + 1) * LANES]
    r_ref[...] = lax.rsqrt(tot * inv_d + EPS)


def row_scale(ss, d_norm, tm):
    m, w = ss.shape
    return pl.pallas_call(
        functools.partial(_row_scale_kernel, inv_d=1.0 / d_norm),
        grid=(m // tm,),
        in_specs=[pl.BlockSpec((tm, w), lambda i: (i, 0))],
        out_specs=pl.BlockSpec((tm, LANES), lambda i: (i, 0)),
        out_shape=jax.ShapeDtypeStruct((m, LANES), F32),
        compiler_params=_params(),
        name="row_scale",
    )(ss)


def _widen(r, width):
    return r if width == LANES else jnp.tile(r, (1, width // LANES))


def _scale_prep_kernel(x_ref, g_ref, xg_ref, r_ref):
    x = x_ref[...]
    xg_ref[...] = (x * g_ref[...]).astype(BF16)
    r = lax.rsqrt(jnp.mean(x * x, axis=-1, keepdims=True) + EPS)
    r_ref[...] = jnp.broadcast_to(r, r_ref.shape)


def scale_prep(x, g, tm):
    m, d = x.shape
    return pl.pallas_call(
        _scale_prep_kernel,
        grid=(m // tm,),
        in_specs=[pl.BlockSpec((tm, d), lambda i: (i, 0)),
                  pl.BlockSpec((1, d), lambda i: (0, 0))],
        out_specs=[pl.BlockSpec((tm, d), lambda i: (i, 0)),
                   pl.BlockSpec((tm, LANES), lambda i: (i, 0))],
        out_shape=[jax.ShapeDtypeStruct((m, d), BF16), jax.ShapeDtypeStruct((m, LANES), F32)],
        compiler_params=_params(),
        name="scale_prep",
    )(x, g.reshape(1, d))


def _mm_cast_kernel(a_ref, b_ref, o_ref):
    acc = jnp.dot(a_ref[...], b_ref[...], preferred_element_type=F32)
    o_ref[...] = acc.astype(o_ref.dtype)


def _mm_swiglu_kernel(a_ref, r_ref, bg_ref, bu_ref, o_ref):
    a = a_ref[...]
    r = _widen(r_ref[...], o_ref.shape[1])
    g = jnp.dot(a, bg_ref[...], preferred_element_type=F32) * r
    u = jnp.dot(a, bu_ref[...], preferred_element_type=F32) * r
    o_ref[...] = (g * (1.0 / (1.0 + jnp.exp(-g))) * u).astype(o_ref.dtype)


def _mm_residual_kernel(*refs, n_a, scale, emit_next):
    a_refs = refs[:n_a]
    b_refs = refs[n_a:2 * n_a]
    r_ref = refs[2 * n_a]
    acc = jnp.dot(a_refs[0][...], b_refs[0][...], preferred_element_type=F32)
    for a_ref, b_ref in zip(a_refs[1:], b_refs[1:]):
        acc = acc + jnp.dot(a_ref[...], b_ref[...], preferred_element_type=F32)
    x = r_ref[...] + (acc if scale == 1.0 else scale * acc)
    if emit_next:
        g_ref = refs[2 * n_a + 1]
        o_ref, xg_ref, ss_ref = refs[-3:]
        xg_ref[...] = (x * g_ref[...]).astype(BF16)
        ss_ref[...] = jnp.broadcast_to(jnp.sum(x * x, axis=-1, keepdims=True), ss_ref.shape)
    else:
        o_ref = refs[-1]
    o_ref[...] = x


def _mm_headnorm_kernel(*refs, cw, inv_count, shifts, scaled):
    a_ref, b_ref = refs[0], refs[1]
    nxt = 2
    acc = jnp.dot(a_ref[...], b_ref[...], preferred_element_type=F32)
    if scaled:
        acc = acc * _widen(refs[2][...], acc.shape[1])
        nxt = 3
    g = refs[nxt][...]
    tabs = refs[nxt + 1:-1]
    o_ref = refs[-1]
    for c in range(acc.shape[1] // cw):
        x = acc[:, c * cw:(c + 1) * cw]
        ms = jnp.sum(x * x, axis=-1, keepdims=True) * inv_count
        y = x * lax.rsqrt(ms + EPS) * g
        if shifts:
            out = y * tabs[0][...]
            for k, sh in enumerate(shifts):
                out = out + pltpu.roll(y, sh, 1) * tabs[1 + k][...]
            y = out
        o_ref[:, c * cw:(c + 1) * cw] = y.astype(o_ref.dtype)


def _r_spec(tm):
    return pl.BlockSpec((tm, LANES), lambda i, j: (i, 0))


def mm_cast(a, b, tm, tn, out_dtype=BF16):
    m, k = a.shape
    n_out = b.shape[1]
    return pl.pallas_call(
        _mm_cast_kernel,
        grid=(m // tm, n_out // tn),
        in_specs=[pl.BlockSpec((tm, k), lambda i, j: (i, 0)),
                  pl.BlockSpec((k, tn), lambda i, j: (0, j))],
        out_specs=pl.BlockSpec((tm, tn), lambda i, j: (i, j)),
        out_shape=jax.ShapeDtypeStruct((m, n_out), out_dtype),
        compiler_params=_params(),
        name="mm_cast",
    )(a, b)


def mm_swiglu(xg, r, w_gu, layer, tm, tn):
    m, k = xg.shape
    d_ff = w_gu.shape[2] // 2
    off = d_ff // tn
    return pl.pallas_call(
        _mm_swiglu_kernel,
        grid=(m // tm, d_ff // tn),
        in_specs=[pl.BlockSpec((tm, k), lambda i, j: (i, 0)),
                  _r_spec(tm),
                  pl.BlockSpec((None, k, tn), lambda i, j: (layer, 0, j)),
                  pl.BlockSpec((None, k, tn), lambda i, j: (layer, 0, j + off))],
        out_specs=pl.BlockSpec((tm, tn), lambda i, j: (i, j)),
        out_shape=jax.ShapeDtypeStruct((m, d_ff), BF16),
        compiler_params=_params(),
        name="mm_swiglu",
    )(xg, r, w_gu, w_gu)


def mm_residual(a_list, w, layer, res, scale, g_next, tm, tn):
    m = a_list[0].shape[0]
    n_out = w.shape[2]
    in_specs = [pl.BlockSpec((tm, a.shape[1]), lambda i, j: (i, 0)) for a in a_list]
    row = 0
    for a in a_list:
        ka = a.shape[1]
        assert row % ka == 0
        in_specs.append(pl.BlockSpec((None, ka, tn), functools.partial(lambda i, j, rb: (layer, rb, j), rb=row // ka)))
        row += ka
    assert row == w.shape[1]
    tile_spec = pl.BlockSpec((tm, tn), lambda i, j: (i, j))
    in_specs.append(tile_spec)
    args = list(a_list) + [w] * len(a_list) + [res]
    emit_next = g_next is not None
    if emit_next:
        in_specs.append(pl.BlockSpec((1, tn), lambda i, j: (0, j)))
        args.append(g_next.reshape(1, n_out))
        out_specs = [tile_spec, tile_spec, pl.BlockSpec((tm, LANES), lambda i, j: (i, j))]
        out_shape = [jax.ShapeDtypeStruct((m, n_out), F32), jax.ShapeDtypeStruct((m, n_out), BF16),
                     jax.ShapeDtypeStruct((m, (n_out // tn) * LANES), F32)]
    else:
        out_specs = tile_spec
        out_shape = jax.ShapeDtypeStruct((m, n_out), F32)
    return pl.pallas_call(
        functools.partial(_mm_residual_kernel, n_a=len(a_list), scale=scale, emit_next=emit_next),
        grid=(m // tm, n_out // tn),
        in_specs=in_specs,
        out_specs=out_specs,
        out_shape=out_shape,
        compiler_params=_params(),
        name="mm_residual",
    )(*args)


def mm_headnorm(a, b, g, cw, count, tm, tn, rope=None, seq_tiles=1, r=None):
    m, k = a.shape
    n_out = b.shape[1]
    in_specs = [pl.BlockSpec((tm, k), lambda i, j: (i, 0)),
                pl.BlockSpec((k, tn), lambda i, j: (0, j))]
    args = [a, b]
    if r is not None:
        in_specs.append(_r_spec(tm))
        args.append(r)
    in_specs.append(pl.BlockSpec((1, cw), lambda i, j: (0, 0)))
    args.append(g.reshape(1, cw))
    shifts = ()
    if rope is not None:
        shifts, tabs = rope
        for t in tabs:
            in_specs.append(pl.BlockSpec((tm, cw), lambda i, j: (i % seq_tiles, 0)))
            args.append(t)
    kern = functools.partial(_mm_headnorm_kernel, cw=cw, inv_count=1.0 / count, shifts=tuple(shifts),
                             scaled=r is not None)
    return pl.pallas_call(
        kern,
        grid=(m // tm, n_out // tn),
        in_specs=in_specs,
        out_specs=pl.BlockSpec((tm, tn), lambda i, j: (i, j)),
        out_shape=jax.ShapeDtypeStruct((m, n_out), BF16),
        compiler_params=_params(),
        name="mm_headnorm",
    )(*args)


def _attn_kernel(*refs, nparts, tk, nkv):
    q_refs = refs[:nparts]
    k_refs = refs[nparts:2 * nparts]
    v_ref = refs[2 * nparts]
    o_ref = refs[2 * nparts + 1]
    if nparts > 1:
        q = jnp.concatenate([r[...] for r in q_refs], axis=-1)
    else:
        q = q_refs[0][...]
    tq = q.shape[0]
    dv = v_ref.shape[-1]

    def body(t, carry):
        m, l, acc = carry
        off = pl.multiple_of(t * tk, tk)
        if nparts > 1:
            k = jnp.concatenate([r[pl.ds(off, tk), :] for r in k_refs], axis=-1)
        else:
            k = k_refs[0][pl.ds(off, tk), :]
        s = lax.dot_general(q, k, (((1,), (1,)), ((), ())), preferred_element_type=F32)
        m_new = jnp.maximum(m, jnp.max(s, axis=-1, keepdims=True))
        p = jnp.exp(s - m_new)
        alpha = jnp.exp(m - m_new)
        l = alpha * l + jnp.sum(p, axis=-1, keepdims=True)
        acc = alpha * acc + jnp.dot(p.astype(BF16), v_ref[pl.ds(off, tk), :], preferred_element_type=F32)
        return m_new, l, acc

    init = (jnp.full((tq, 1), -jnp.inf, F32), jnp.zeros((tq, 1), F32), jnp.zeros((tq, dv), F32))
    if nkv == 1:
        m, l, acc = body(0, init)
    else:
        m, l, acc = lax.fori_loop(0, nkv, body, init)
    o_ref[...] = (acc / l).astype(o_ref.dtype)


def attention(qs, ks, v, n_heads, group, dq, dv, tq, tk):
    b, s, _ = qs[0].shape
    skv = v.shape[1]
    nparts = len(qs)
    in_specs = [pl.BlockSpec((None, tq, dq), lambda bi, h, qi: (bi, qi, h)) for _ in qs]
    for k in ks:
        if k.shape[2] == dq:
            in_specs.append(pl.BlockSpec((None, skv, dq), lambda bi, h, qi: (bi, 0, 0)))
        else:
            in_specs.append(pl.BlockSpec((None, skv, dq), lambda bi, h, qi: (bi, 0, h // group)))
    in_specs.append(pl.BlockSpec((None, skv, dv), lambda bi, h, qi: (bi, 0, h // group)))
    kern = functools.partial(_attn_kernel, nparts=nparts, tk=tk, nkv=skv // tk)
    return pl.pallas_call(
        kern,
        grid=(b, n_heads, s // tq),
        in_specs=in_specs,
        out_specs=pl.BlockSpec((None, tq, dv), lambda bi, h, qi: (bi, qi, h)),
        out_shape=jax.ShapeDtypeStruct((b, s, n_heads * dv), BF16),
        compiler_params=_params(),
        name="attention",
    )(*qs, *ks, v)


def _attn_t_kernel(*refs, nparts, tk, nkv, n_sub):
    q_refs = refs[:nparts]
    k_refs = refs[nparts:2 * nparts]
    vt_ref = refs[2 * nparts]
    o_ref = refs[2 * nparts + 1]
    scratch = refs[2 * nparts + 2:]
    dv = vt_ref.shape[1]
    tq = o_ref.shape[0] // n_sub
    first_q_tile = pl.program_id(2) == 0
    if nparts > 1:
        kcat_ref, vx_ref, s_ref, m_ref, acc_ref = scratch

        @pl.when(first_q_tile)
        def _():
            for i, r in enumerate(k_refs):
                kcat_ref[:, i * HEAD_DIM:(i + 1) * HEAD_DIM] = r[...]
        k_ref = kcat_ref
    else:
        vx_ref, s_ref, m_ref, acc_ref = scratch
        k_ref = k_refs[0]

    @pl.when(first_q_tile)
    def _():
        vx_ref[:, :dv, :] = vt_ref[...]
        vx_ref[:, dv:, :] = jnp.ones((nkv, ONES_ROWS, tk), BF16)

    def q_t(t):
        parts = [r[t * tq:(t + 1) * tq, :] for r in q_refs]
        q = parts[0] if nparts == 1 else jnp.concatenate(parts, axis=-1)
        return q.astype(F32).T.astype(BF16)

    qts = [q_t(t) for t in range(n_sub)]

    def qk(t, c, slot):
        off = pl.multiple_of(c * tk, tk)
        s_ref[slot] = jnp.dot(k_ref[pl.ds(off, tk), :], qts[t], preferred_element_type=F32)

    def softmax_pv(t, c, slot):
        s = s_ref[slot]
        m_old = m_ref[t]
        m_new = jnp.maximum(m_old, jnp.max(s, axis=0, keepdims=True))
        alpha = jnp.exp2(m_old - m_new)
        p = jnp.exp2(s - m_new)
        m_ref[t] = m_new
        acc_ref[t] = alpha * acc_ref[t] + jnp.dot(vx_ref[c], p.astype(BF16), preferred_element_type=F32)

    def finish(t):
        out = acc_ref[t, :dv, :] * (1.0 / acc_ref[t, dv:dv + 1, :])
        o_ref[t * tq:(t + 1) * tq, :] = out.T.astype(o_ref.dtype)

    m_ref[...] = jnp.full(m_ref.shape, -jnp.inf, F32)
    acc_ref[...] = jnp.zeros(acc_ref.shape, F32)

    if n_sub * nkv <= MAX_STATIC_UNITS:
        units = [(t, c) for t in range(n_sub) for c in range(nkv)]
        qk(*units[0], 0)
        for u, (t, c) in enumerate(units):
            if u + 1 < len(units):
                qk(*units[u + 1], (u + 1) % 2)
            softmax_pv(t, c, u % 2)
            if c == nkv - 1:
                finish(t)
    else:
        assert n_sub == 1
        qk(0, 0, 0)

        def pair(i, carry):
            c = 2 * i
            qk(0, c + 1, 1)
            softmax_pv(0, c, 0)
            qk(0, c + 2, 0)
            softmax_pv(0, c + 1, 1)
            return carry

        lax.fori_loop(0, nkv // 2 - 1, pair, 0)
        qk(0, nkv - 1, 1)
        softmax_pv(0, nkv - 2, 0)
        softmax_pv(0, nkv - 1, 1)
        finish(0)


def attention_t(qs, ks, vt, n_heads, group, tq, n_sub):
    b, s, _ = qs[0].shape
    skv = ks[0].shape[1]
    nkv, tk = vt.shape[1], vt.shape[3]
    n_kv = n_heads // group
    dv = vt.shape[2] // n_kv
    nparts = len(qs)
    tqb = tq * n_sub
    assert nkv % 2 == 0 and nkv * tk == skv and s % tqb == 0
    in_specs = [pl.BlockSpec((None, tqb, HEAD_DIM), lambda bi, h, qi: (bi, qi, h)) for _ in qs]
    for k in ks:
        if k.shape[2] == n_kv * HEAD_DIM:
            in_specs.append(pl.BlockSpec((None, skv, HEAD_DIM), lambda bi, h, qi: (bi, 0, h // group)))
        else:
            in_specs.append(pl.BlockSpec((None, skv, HEAD_DIM), lambda bi, h, qi: (bi, 0, 0)))
    in_specs.append(pl.BlockSpec((None, nkv, dv, tk), lambda bi, h, qi: (bi, 0, h // group, 0)))
    scratch = []
    if nparts > 1:
        scratch.append(pltpu.VMEM((skv, nparts * HEAD_DIM), BF16))
    scratch += [pltpu.VMEM((nkv, dv + ONES_ROWS, tk), BF16), pltpu.VMEM((2, tk, tq), F32),
                pltpu.VMEM((n_sub, 1, tq), F32), pltpu.VMEM((n_sub, dv + ONES_ROWS, tq), F32)]
    kern = functools.partial(_attn_t_kernel, nparts=nparts, tk=tk, nkv=nkv, n_sub=n_sub)
    return pl.pallas_call(
        kern,
        grid=(b, n_heads, s // tqb),
        in_specs=in_specs,
        out_specs=pl.BlockSpec((None, tqb, dv), lambda bi, h, qi: (bi, qi, h)),
        out_shape=jax.ShapeDtypeStruct((b, s, n_heads * dv), BF16),
        scratch_shapes=scratch,
        compiler_params=_params(),
        name="attention_t",
    )(*qs, *ks, vt)


def _mm_vt_kernel(*refs, scaled):
    a_ref, wt_ref, o_ref = refs[0], refs[1], refs[-1]
    out = lax.dot_general(wt_ref[...], a_ref[...], (((1,), (1,)), ((), ())), preferred_element_type=F32)
    if scaled:
        out = out * refs[2][...].T[:1, :]
    o_ref[...] = out.astype(o_ref.dtype)


def mm_vt(a, wt, tk, tn, r=None):
    m, k = a.shape
    cols = wt.shape[0]
    tr = min(tk, 512)
    per = tk // tr
    in_specs = [pl.BlockSpec((tr, k), lambda i, j: (i, 0)),
                pl.BlockSpec((tn, k), lambda i, j: (j, 0))]
    args = [a, wt]
    if r is not None:
        in_specs.append(_r_spec(tr))
        args.append(r)
    return pl.pallas_call(
        functools.partial(_mm_vt_kernel, scaled=r is not None),
        grid=(m // tr, cols // tn),
        in_specs=in_specs,
        out_specs=pl.BlockSpec((None, tn, tr), lambda i, j: (i // per, j, i % per)),
        out_shape=jax.ShapeDtypeStruct((m // tk, cols, tk), BF16),
        compiler_params=_params(),
        name="mm_vt",
    )(*args)


def _deinterleave(w, width):
    lead = w.shape[:-1]
    n = w.shape[-1] // width
    w = w.reshape(lead + (n, width // 2, 2))
    w = jnp.swapaxes(w, -1, -2)
    return w.reshape(lead + (n * width,))


def _pad_groups(w, width, to):
    lead = w.shape[:-1]
    n = w.shape[-1] // width
    w = w.reshape(lead + (n, width))
    w = jnp.pad(w, [(0, 0)] * len(lead) + [(0, 0), (0, to - width)])
    return w.reshape(lead + (n * to,))


def _rope_angles(seq_len, rot_dim):
    rows = seq_len // GRID_W
    row, col = jnp.meshgrid(jnp.arange(rows), jnp.arange(GRID_W), indexing="ij")
    row = row.reshape(-1).astype(F32)
    col = col.reshape(-1).astype(F32)
    axis_dim = rot_dim // 2
    inv_freq = ROPE_THETA ** (-jnp.arange(0, axis_dim, 2, dtype=F32) / axis_dim)
    ang = jnp.concatenate([row[:, None] * inv_freq, col[:, None] * inv_freq], axis=-1)
    return jnp.cos(ang), jnp.sin(ang)


def _rope_tables(seq_len, rot_dim, width):
    cos, sin = _rope_angles(seq_len, rot_dim)
    half = rot_dim // 2
    z = jnp.zeros((seq_len, width - rot_dim), F32)
    zh = jnp.zeros_like(sin)
    c = jnp.concatenate([cos, cos, z], axis=-1)
    if rot_dim == width:
        return (half,), [c, jnp.concatenate([-sin, sin], axis=-1)]
    s_a = jnp.concatenate([-sin, zh, z], axis=-1)
    s_b = jnp.concatenate([zh, sin, z], axis=-1)
    return (width - half, half), [c, s_a, s_b]


def kernel(x, mem, ffn1_norm, ffn1_w_gu, ffn1_w_down, mix_norm, w_o, mem_norm, w_mem_kv, mem_q_norm, mem_k_norm, ffn2_norm, ffn2_w_gu, ffn2_w_down, a_w_in, a_q_a_norm, a_kv_a_norm, a_w_q_b, a_w_kv_b, a_q_nope_norm, a_q_pe_norm, a_k_nope_norm, a_k_pe_norm, b_w_in, b_q_norm, b_k_norm):
    b, s, d = x.shape
    n = b * s
    m_tok = mem.shape[1]
    depth = ffn1_norm.shape[0]
    hd = HEAD_DIM
    mem_w = w_mem_kv.shape[2] // 2
    mem_hd = mem_w // MEM_HEADS
    tok_w = w_o.shape[1] - mem_w
    n_heads = tok_w // hd
    q_lora = a_q_a_norm.shape[1]
    kv_lora = a_kv_a_norm.shape[1]
    n_kv = n_heads // B_GROUP

    tm = min(1024, s)
    seq_tiles = s // tm
    tq = min(1024, s)
    tk = min(2048, s // 4)
    n_sub = 2 if s // tq >= 2 else 1
    tm_mem = min(512, b * m_tok)
    tm_norm = min(256, s)

    def tile(width, pref):
        t = min(pref, width)
        while width % t:
            t -= hd
        return t

    rope_a = _rope_tables(s, A_ROPE, hd)
    rope_b = _rope_tables(s, hd, hd)
    scale_a = float((hd + A_ROPE) ** -0.5) * LOG2E
    scale_b = float(hd ** -0.5) * LOG2E
    scale_m = float(mem_hd ** -0.5)

    xf = x.reshape(n, d)
    memf = mem.reshape(b * m_tok, d)
    w_gu1, w_gu2 = ffn1_w_gu.astype(BF16), ffn2_w_gu.astype(BF16)
    w_dn1, w_dn2 = ffn1_w_down.astype(BF16), ffn2_w_down.astype(BF16)
    w_ob = w_o.astype(BF16)
    d_ff = w_gu1.shape[2] // 2
    r3 = lambda t: t.reshape(b, s, t.shape[-1])

    def produce(a_list, w, scale, g_next):
        x_new, xg, ss = mm_residual(a_list, w, i, xf, scale, g_next, tm, tile(d, 512))
        return x_new, xg, row_scale(ss, d, tm)

    xg, r = scale_prep(xf, ffn1_norm[0], tm_norm)
    for i in range(depth):
        act = mm_swiglu(xg, r, w_gu1, i, tm, tile(d_ff, 512))
        xf, xg, r = produce([act], w_dn1, 0.5, mix_norm[i])
        proj = functools.partial(mm_headnorm, r=r)
        j = i // N_MIXERS
        if i % N_MIXERS == 0:
            w_in = a_w_in[j]
            o1, o2, o3 = q_lora, q_lora + kv_lora, q_lora + kv_lora + A_ROPE
            w_cq = w_in[:, :o1].astype(BF16)
            w_ckv = w_in[:, o1:o2].astype(BF16)
            w_kpe = _pad_groups(_deinterleave(w_in[:, o2:o3], A_ROPE), A_ROPE, hd).astype(BF16)
            w_qm = w_in[:, o3:].astype(BF16)
            wq = a_w_q_b[j].reshape(q_lora, n_heads, hd + A_ROPE)
            w_qn = wq[:, :, :hd].reshape(q_lora, n_heads * hd).astype(BF16)
            w_qp = _pad_groups(_deinterleave(wq[:, :, hd:].reshape(q_lora, n_heads * A_ROPE), A_ROPE),
                               A_ROPE, hd).astype(BF16)
            wkv = a_w_kv_b[j].reshape(kv_lora, n_heads, 2 * hd)
            w_kn = wkv[:, :, :hd].reshape(kv_lora, n_heads * hd).astype(BF16)
            w_v = wkv[:, :, hd:].reshape(kv_lora, n_heads * hd).astype(BF16)
            g_qpe = _pad_groups(_deinterleave(a_q_pe_norm[j], A_ROPE), A_ROPE, hd) * scale_a
            g_kpe = _pad_groups(_deinterleave(a_k_pe_norm[j], A_ROPE), A_ROPE, hd)

            cqn = proj(xg, w_cq, a_q_a_norm[j], q_lora, q_lora, tm, q_lora)
            ckvn = proj(xg, w_ckv, a_kv_a_norm[j], kv_lora, kv_lora, tm, kv_lora)
            kpe = proj(xg, w_kpe, g_kpe, hd, A_ROPE, tm, hd, rope=rope_a, seq_tiles=seq_tiles)
            mq = proj(xg, w_qm, mem_q_norm[i] * scale_m, mem_hd, mem_hd, tm, tile(mem_w, 1024))
            tw = tile(n_heads * hd, 1024)
            qn = mm_headnorm(cqn, w_qn, a_q_nope_norm[j] * scale_a, hd, hd, tm, tw)
            qp = mm_headnorm(cqn, w_qp, g_qpe, hd, A_ROPE, tm, tw, rope=rope_a, seq_tiles=seq_tiles)
            kn = mm_headnorm(ckvn, w_kn, a_k_nope_norm[j], hd, hd, tm, tw)
            vt = mm_vt(ckvn, w_v.T, tk, tw).reshape(b, s // tk, n_heads * hd, tk)
            tok = attention_t([r3(qn), r3(qp)], [r3(kn), r3(kpe)], vt, n_heads, 1, tq, n_sub)
        else:
            w_in = b_w_in[j]
            qw, kw = n_heads * hd, n_kv * hd
            w_q = _deinterleave(w_in[:, :qw], hd).astype(BF16)
            w_k = _deinterleave(w_in[:, qw:qw + kw], hd).astype(BF16)
            w_v = w_in[:, qw + kw:qw + 2 * kw].astype(BF16)
            w_qm = w_in[:, qw + 2 * kw:].astype(BF16)
            g_q = _deinterleave(b_q_norm[j], hd) * scale_b
            g_k = _deinterleave(b_k_norm[j], hd)
            q = proj(xg, w_q, g_q, hd, hd, tm, tile(qw, 1024), rope=rope_b, seq_tiles=seq_tiles)
            k = proj(xg, w_k, g_k, hd, hd, tm, tile(kw, 1024), rope=rope_b, seq_tiles=seq_tiles)
            vt = mm_vt(xg, w_v.T, tk, tile(kw, 1024), r=r).reshape(b, s // tk, kw, tk)
            mq = proj(xg, w_qm, mem_q_norm[i] * scale_m, mem_hd, mem_hd, tm, tile(mem_w, 1024))
            tok = attention_t([r3(q)], [r3(k)], vt, n_heads, B_GROUP, tq, n_sub)

        memn = rmsnorm(memf, mem_norm[i], tm_mem)
        w_mkv = w_mem_kv[i].astype(BF16)
        mk = mm_headnorm(memn, w_mkv[:, :mem_w], mem_k_norm[i], mem_hd, mem_hd, tm_mem, tile(mem_w, 1024))
        mv = mm_cast(memn, w_mkv[:, mem_w:], tm_mem, tile(mem_w, 1024))
        mo = attention([mq.reshape(b, s, mem_w)], [mk.reshape(b, m_tok, mem_w)], mv.reshape(b, m_tok, mem_w),
                       MEM_HEADS, 1, mem_hd, mem_hd, tq, m_tok)
        xf, xg, r = produce([tok.reshape(n, tok_w), mo.reshape(n, mem_w)], w_ob, 1.0, ffn2_norm[i])
        act = mm_swiglu(xg, r, w_gu2, i, tm, tile(d_ff, 512))
        if i + 1 < depth:
            xf, xg, r = produce([act], w_dn2, 0.5, ffn1_norm[i + 1])
        else:
            xf = mm_residual([act], w_dn2, i, xf, 0.5, None, tm, tile(d, 512))
    return xf.reshape(b, s, d)
```

```python
import functools

import jax
import jax.numpy as jnp
from jax import lax
from jax.experimental import pallas as pl
from jax.experimental.pallas import tpu as pltpu

EPS = 1e-6
GRID_W = 64
ROPE_THETA = 10000.0
LANES = 128
HEAD_DIM = LANES
A_ROPE = 64
MEM_HEADS = 4
LOG2E = 1.4426950408889634
N_MIXERS = 2
B_GROUP = 4
ONES_ROWS = 16
MAX_STATIC_UNITS = 8
V7X_VMEM_BYTES = 64 * 1024 * 1024
VMEM_LIMIT = V7X_VMEM_BYTES * 7 // 8

F32 = jnp.float32
BF16 = jnp.bfloat16


def _params():
    return pltpu.CompilerParams(vmem_limit_bytes=VMEM_LIMIT)


def _rmsnorm_kernel(x_ref, g_ref, o_ref):
    x = x_ref[...]
    ms = jnp.mean(x * x, axis=-1, keepdims=True)
    o_ref[...] = (x * lax.rsqrt(ms + EPS) * g_ref[...]).astype(o_ref.dtype)


def rmsnorm(x, g, tm):
    m, d = x.shape
    return pl.pallas_call(
        _rmsnorm_kernel,
        grid=(m // tm,),
        in_specs=[pl.BlockSpec((tm, d), lambda i: (i, 0)),
                  pl.BlockSpec((1, d), lambda i: (0, 0))],
        out_specs=pl.BlockSpec((tm, d), lambda i: (i, 0)),
        out_shape=jax.ShapeDtypeStruct((m, d), BF16),
        compiler_params=_params(),
        name="rmsnorm",
    )(x, g.reshape(1, d))


def _row_scale_kernel(ss_ref, r_ref, *, inv_d):
    ss = ss_ref[...]
    tot = ss[:, :LANES]
    for c in range(1, ss.shape[1] // LANES):
        tot = tot + ss[:, c * LANES:(c + 1) * LANES]
    r_ref[...] = lax.rsqrt(tot * inv_d + EPS)


def row_scale(ss, d_norm, tm):
    m, w = ss.shape
    return pl.pallas_call(
        functools.partial(_row_scale_kernel, inv_d=1.0 / d_norm),
        grid=(m // tm,),
        in_specs=[pl.BlockSpec((tm, w), lambda i: (i, 0))],
        out_specs=pl.BlockSpec((tm, LANES), lambda i: (i, 0)),
        out_shape=jax.ShapeDtypeStruct((m, LANES), F32),
        compiler_params=_params(),
        name="row_scale",
    )(ss)


def _widen(r, width):
    return r if width == LANES else jnp.tile(r, (1, width // LANES))


def _scale_prep_kernel(x_ref, g_ref, xg_ref, r_ref):
    x = x_ref[...]
    xg_ref[...] = (x * g_ref[...]).astype(BF16)
    r = lax.rsqrt(jnp.mean(x * x, axis=-1, keepdims=True) + EPS)
    r_ref[...] = jnp.broadcast_to(r, r_ref.shape)


def scale_prep(x, g, tm):
    m, d = x.shape
    return pl.pallas_call(
        _scale_prep_kernel,
        grid=(m // tm,),
        in_specs=[pl.BlockSpec((tm, d), lambda i: (i, 0)),
                  pl.BlockSpec((1, d), lambda i: (0, 0))],
        out_specs=[pl.BlockSpec((tm, d), lambda i: (i, 0)),
                   pl.BlockSpec((tm, LANES), lambda i: (i, 0))],
        out_shape=[jax.ShapeDtypeStruct((m, d), BF16), jax.ShapeDtypeStruct((m, LANES), F32)],
        compiler_params=_params(),
        name="scale_prep",
    )(x, g.reshape(1, d))


def _mm_cast_kernel(*refs, scaled):
    a_ref, b_ref, o_ref = refs[0], refs[1], refs[-1]
    acc = jnp.dot(a_ref[...], b_ref[...], preferred_element_type=F32)
    if scaled:
        acc = acc * _widen(refs[2][...], acc.shape[1])
    o_ref[...] = acc.astype(o_ref.dtype)


def _mm_swiglu_kernel(a_ref, r_ref, bg_ref, bu_ref, o_ref):
    a = a_ref[...]
    r = _widen(r_ref[...], o_ref.shape[1])
    g = jnp.dot(a, bg_ref[...], preferred_element_type=F32) * r
    u = jnp.dot(a, bu_ref[...], preferred_element_type=F32) * r
    o_ref[...] = (g * (1.0 / (1.0 + jnp.exp(-g))) * u).astype(o_ref.dtype)


def _mm_residual_kernel(*refs, n_a, scale, emit_next):
    a_refs = refs[:n_a]
    b_refs = refs[n_a:2 * n_a]
    r_ref = refs[2 * n_a]
    acc = jnp.dot(a_refs[0][...], b_refs[0][...], preferred_element_type=F32)
    for a_ref, b_ref in zip(a_refs[1:], b_refs[1:]):
        acc = acc + jnp.dot(a_ref[...], b_ref[...], preferred_element_type=F32)
    x = r_ref[...] + (acc if scale == 1.0 else scale * acc)
    if emit_next:
        g_ref = refs[2 * n_a + 1]
        o_ref, xg_ref, ss_ref = refs[-3:]
        xg_ref[...] = (x * g_ref[...]).astype(BF16)
        ss_ref[...] = jnp.broadcast_to(jnp.sum(x * x, axis=-1, keepdims=True), ss_ref.shape)
    else:
        o_ref = refs[-1]
    o_ref[...] = x


def _mm_headnorm_kernel(*refs, cw, inv_count, shifts, scaled):
    a_ref, b_ref = refs[0], refs[1]
    nxt = 2
    acc = jnp.dot(a_ref[...], b_ref[...], preferred_element_type=F32)
    if scaled:
        acc = acc * _widen(refs[2][...], acc.shape[1])
        nxt = 3
    g = refs[nxt][...]
    tabs = refs[nxt + 1:-1]
    o_ref = refs[-1]
    for c in range(acc.shape[1] // cw):
        x = acc[:, c * cw:(c + 1) * cw]
        ms = jnp.sum(x * x, axis=-1, keepdims=True) * inv_count
        y = x * lax.rsqrt(ms + EPS) * g
        if shifts:
            out = y * tabs[0][...]
            for k, sh in enumerate(shifts):
                out = out + pltpu.roll(y, sh, 1) * tabs[1 + k][...]
            y = out
        o_ref[:, c * cw:(c + 1) * cw] = y.astype(o_ref.dtype)


def _r_spec(tm):
    return pl.BlockSpec((tm, LANES), lambda i, j: (i, 0))


def mm_cast(a, b, tm, tn, out_dtype=BF16, r=None):
    m, k = a.shape
    n_out = b.shape[1]
    in_specs = [pl.BlockSpec((tm, k), lambda i, j: (i, 0)),
                pl.BlockSpec((k, tn), lambda i, j: (0, j))]
    args = [a, b]
    if r is not None:
        in_specs.append(_r_spec(tm))
        args.append(r)
    return pl.pallas_call(
        functools.partial(_mm_cast_kernel, scaled=r is not None),
        grid=(m // tm, n_out // tn),
        in_specs=in_specs,
        out_specs=pl.BlockSpec((tm, tn), lambda i, j: (i, j)),
        out_shape=jax.ShapeDtypeStruct((m, n_out), out_dtype),
        compiler_params=_params(),
        name="mm_cast",
    )(*args)


def mm_swiglu(xg, r, w_gu, layer, tm, tn):
    m, k = xg.shape
    d_ff = w_gu.shape[2] // 2
    off = d_ff // tn
    return pl.pallas_call(
        _mm_swiglu_kernel,
        grid=(m // tm, d_ff // tn),
        in_specs=[pl.BlockSpec((tm, k), lambda i, j: (i, 0)),
                  _r_spec(tm),
                  pl.BlockSpec((None, k, tn), lambda i, j: (layer, 0, j)),
                  pl.BlockSpec((None, k, tn), lambda i, j: (layer, 0, j + off))],
        out_specs=pl.BlockSpec((tm, tn), lambda i, j: (i, j)),
        out_shape=jax.ShapeDtypeStruct((m, d_ff), BF16),
        compiler_params=_params(),
        name="mm_swiglu",
    )(xg, r, w_gu, w_gu)


def mm_residual(a_list, w, layer, res, scale, g_next, tm, tn):
    m = a_list[0].shape[0]
    n_out = w.shape[2]
    in_specs = [pl.BlockSpec((tm, a.shape[1]), lambda i, j: (i, 0)) for a in a_list]
    row = 0
    for a in a_list:
        ka = a.shape[1]
        assert row % ka == 0
        in_specs.append(pl.BlockSpec((None, ka, tn), functools.partial(lambda i, j, rb: (layer, rb, j), rb=row // ka)))
        row += ka
    assert row == w.shape[1]
    tile_spec = pl.BlockSpec((tm, tn), lambda i, j: (i, j))
    in_specs.append(tile_spec)
    args = list(a_list) + [w] * len(a_list) + [res]
    emit_next = g_next is not None
    if emit_next:
        in_specs.append(pl.BlockSpec((1, tn), lambda i, j: (0, j)))
        args.append(g_next.reshape(1, n_out))
        out_specs = [tile_spec, tile_spec, pl.BlockSpec((tm, LANES), lambda i, j: (i, j))]
        out_shape = [jax.ShapeDtypeStruct((m, n_out), F32), jax.ShapeDtypeStruct((m, n_out), BF16),
                     jax.ShapeDtypeStruct((m, (n_out // tn) * LANES), F32)]
    else:
        out_specs = tile_spec
        out_shape = jax.ShapeDtypeStruct((m, n_out), F32)
    return pl.pallas_call(
        functools.partial(_mm_residual_kernel, n_a=len(a_list), scale=scale, emit_next=emit_next),
        grid=(m // tm, n_out // tn),
        in_specs=in_specs,
        out_specs=out_specs,
        out_shape=out_shape,
        compiler_params=_params(),
        name="mm_residual",
    )(*args)


def mm_headnorm(a, b, g, cw, count, tm, tn, rope=None, seq_tiles=1, r=None):
    m, k = a.shape
    n_out = b.shape[1]
    in_specs = [pl.BlockSpec((tm, k), lambda i, j: (i, 0)),
                pl.BlockSpec((k, tn), lambda i, j: (0, j))]
    args = [a, b]
    if r is not None:
        in_specs.append(_r_spec(tm))
        args.append(r)
    in_specs.append(pl.BlockSpec((1, cw), lambda i, j: (0, 0)))
    args.append(g.reshape(1, cw))
    shifts = ()
    if rope is not None:
        shifts, tabs = rope
        for t in tabs:
            in_specs.append(pl.BlockSpec((tm, cw), lambda i, j: (i % seq_tiles, 0)))
            args.append(t)
    kern = functools.partial(_mm_headnorm_kernel, cw=cw, inv_count=1.0 / count, shifts=tuple(shifts),
                             scaled=r is not None)
    return pl.pallas_call(
        kern,
        grid=(m // tm, n_out // tn),
        in_specs=in_specs,
        out_specs=pl.BlockSpec((tm, tn), lambda i, j: (i, j)),
        out_shape=jax.ShapeDtypeStruct((m, n_out), BF16),
        compiler_params=_params(),
        name="mm_headnorm",
    )(*args)


def _attn_kernel(*refs, nparts, tk, nkv):
    q_refs = refs[:nparts]
    k_refs = refs[nparts:2 * nparts]
    v_ref = refs[2 * nparts]
    o_ref = refs[2 * nparts + 1]
    if nparts > 1:
        q = jnp.concatenate([r[...] for r in q_refs], axis=-1)
    else:
        q = q_refs[0][...]
    tq = q.shape[0]
    dv = v_ref.shape[-1]

    def body(t, carry):
        m, l, acc = carry
        off = pl.multiple_of(t * tk, tk)
        if nparts > 1:
            k = jnp.concatenate([r[pl.ds(off, tk), :] for r in k_refs], axis=-1)
        else:
            k = k_refs[0][pl.ds(off, tk), :]
        s = lax.dot_general(q, k, (((1,), (1,)), ((), ())), preferred_element_type=F32)
        m_new = jnp.maximum(m, jnp.max(s, axis=-1, keepdims=True))
        p = jnp.exp(s - m_new)
        alpha = jnp.exp(m - m_new)
        l = alpha * l + jnp.sum(p, axis=-1, keepdims=True)
        acc = alpha * acc + jnp.dot(p.astype(BF16), v_ref[pl.ds(off, tk), :], preferred_element_type=F32)
        return m_new, l, acc

    init = (jnp.full((tq, 1), -jnp.inf, F32), jnp.zeros((tq, 1), F32), jnp.zeros((tq, dv), F32))
    if nkv == 1:
        m, l, acc = body(0, init)
    else:
        m, l, acc = lax.fori_loop(0, nkv, body, init)
    o_ref[...] = (acc / l).astype(o_ref.dtype)


def attention(qs, ks, v, n_heads, group, dq, dv, tq, tk):
    b, s, _ = qs[0].shape
    skv = v.shape[1]
    nparts = len(qs)
    in_specs = [pl.BlockSpec((None, tq, dq), lambda bi, h, qi: (bi, qi, h)) for _ in qs]
    for k in ks:
        if k.shape[2] == dq:
            in_specs.append(pl.BlockSpec((None, skv, dq), lambda bi, h, qi: (bi, 0, 0)))
        else:
            in_specs.append(pl.BlockSpec((None, skv, dq), lambda bi, h, qi: (bi, 0, h // group)))
    in_specs.append(pl.BlockSpec((None, skv, dv), lambda bi, h, qi: (bi, 0, h // group)))
    kern = functools.partial(_attn_kernel, nparts=nparts, tk=tk, nkv=skv // tk)
    return pl.pallas_call(
        kern,
        grid=(b, n_heads, s // tq),
        in_specs=in_specs,
        out_specs=pl.BlockSpec((None, tq, dv), lambda bi, h, qi: (bi, qi, h)),
        out_shape=jax.ShapeDtypeStruct((b, s, n_heads * dv), BF16),
        compiler_params=_params(),
        name="attention",
    )(*qs, *ks, v)


def _norm_rows(x, g_ref):
    y = x * lax.rsqrt(jnp.mean(x * x, axis=0, keepdims=True) + EPS)
    return y * jnp.tile(g_ref[...], (1, x.shape[1] // LANES))


def _rope_rows(y, c, s):
    half = y.shape[0] // 2
    x0, x1 = y[:half], y[half:]
    return x0 * c - x1 * s, x0 * s + x1 * c


def _attn_t_kernel(*refs, nparts, tk, nkv, n_sub):
    q_refs = refs[:nparts]
    g_refs = refs[nparts:2 * nparts]
    cos_ref, sin_ref = refs[2 * nparts], refs[2 * nparts + 1]
    k_refs = refs[2 * nparts + 2:3 * nparts + 2]
    vt_ref = refs[3 * nparts + 2]
    o_ref = refs[3 * nparts + 3]
    scratch = refs[3 * nparts + 4:]
    dv = vt_ref.shape[1]
    tq = o_ref.shape[0] // n_sub
    first_q_tile = pl.program_id(2) == 0
    if nparts > 1:
        kcat_ref, qp_ref, vx_ref, s_ref, m_ref, acc_ref = scratch

        @pl.when(first_q_tile)
        def _():
            for i, r in enumerate(k_refs):
                kcat_ref[:, i * HEAD_DIM:(i + 1) * HEAD_DIM] = r[...]
        k_ref = kcat_ref
    else:
        vx_ref, s_ref, m_ref, acc_ref = scratch
        k_ref = k_refs[0]

    @pl.when(first_q_tile)
    def _():
        vx_ref[:, :dv, :] = vt_ref[...]
        vx_ref[:, dv:, :] = jnp.ones((nkv, ONES_ROWS, tk), BF16)

    def q_t(t):
        rows = slice(t * tq, (t + 1) * tq)
        c, sn = cos_ref[:, rows], sin_ref[:, rows]
        if nparts == 1:
            out = _rope_rows(_norm_rows(q_refs[0][rows, :].T, g_refs[0]), c, sn)
        else:
            qp_ref[...] = q_refs[1][rows, :].T
            off = pl.multiple_of((pl.program_id(1) % 2) * A_ROPE, A_ROPE)
            o0, o1 = _rope_rows(_norm_rows(qp_ref[pl.ds(off, A_ROPE), :], g_refs[1]), c, sn)
            out = (_norm_rows(q_refs[0][rows, :].T, g_refs[0]), o0, o1, jnp.zeros((A_ROPE, tq), F32))
        return jnp.concatenate(out, axis=0).astype(BF16)

    qts = [q_t(t) for t in range(n_sub)]

    def qk(t, c, slot):
        off = pl.multiple_of(c * tk, tk)
        s_ref[slot] = jnp.dot(k_ref[pl.ds(off, tk), :], qts[t], preferred_element_type=F32)

    def softmax_pv(t, c, slot):
        s = s_ref[slot]
        m_old = m_ref[t]
        m_new = jnp.maximum(m_old, jnp.max(s, axis=0, keepdims=True))
        alpha = jnp.exp2(m_old - m_new)
        p = jnp.exp2(s - m_new)
        m_ref[t] = m_new
        acc_ref[t] = alpha * acc_ref[t] + jnp.dot(vx_ref[c], p.astype(BF16), preferred_element_type=F32)

    def finish(t):
        out = acc_ref[t, :dv, :] * (1.0 / acc_ref[t, dv:dv + 1, :])
        o_ref[t * tq:(t + 1) * tq, :] = out.T.astype(o_ref.dtype)

    m_ref[...] = jnp.full(m_ref.shape, -jnp.inf, F32)
    acc_ref[...] = jnp.zeros(acc_ref.shape, F32)

    if n_sub * nkv <= MAX_STATIC_UNITS:
        units = [(t, c) for t in range(n_sub) for c in range(nkv)]
        qk(*units[0], 0)
        for u, (t, c) in enumerate(units):
            if u + 1 < len(units):
                qk(*units[u + 1], (u + 1) % 2)
            softmax_pv(t, c, u % 2)
            if c == nkv - 1:
                finish(t)
    else:
        assert n_sub == 1
        qk(0, 0, 0)

        def pair(i, carry):
            c = 2 * i
            qk(0, c + 1, 1)
            softmax_pv(0, c, 0)
            qk(0, c + 2, 0)
            softmax_pv(0, c + 1, 1)
            return carry

        lax.fori_loop(0, nkv // 2 - 1, pair, 0)
        qk(0, nkv - 1, 1)
        softmax_pv(0, nkv - 2, 0)
        softmax_pv(0, nkv - 1, 1)
        finish(0)


def attention_t(qs, gs, cos_t, sin_t, ks, vt, n_heads, group, tq, n_sub):
    b, s, _ = qs[0].shape
    skv = ks[0].shape[1]
    nkv, tk = vt.shape[1], vt.shape[3]
    n_kv = n_heads // group
    dv = vt.shape[2] // n_kv
    nparts = len(qs)
    tqb = tq * n_sub
    assert nkv % 2 == 0 and nkv * tk == skv and s % tqb == 0
    in_specs = [pl.BlockSpec((None, tqb, HEAD_DIM), lambda bi, h, qi: (bi, qi, h))]
    if nparts > 1:
        in_specs.append(pl.BlockSpec((None, tqb, HEAD_DIM), lambda bi, h, qi: (bi, qi, h // 2)))
    in_specs += [pl.BlockSpec(g.shape, lambda bi, h, qi: (0, 0)) for g in gs]
    in_specs += [pl.BlockSpec((cos_t.shape[0], tqb), lambda bi, h, qi: (0, qi))] * 2
    for k in ks:
        if k.shape[2] == n_kv * HEAD_DIM:
            in_specs.append(pl.BlockSpec((None, skv, HEAD_DIM), lambda bi, h, qi: (bi, 0, h // group)))
        else:
            in_specs.append(pl.BlockSpec((None, skv, HEAD_DIM), lambda bi, h, qi: (bi, 0, 0)))
    in_specs.append(pl.BlockSpec((None, nkv, dv, tk), lambda bi, h, qi: (bi, 0, h // group, 0)))
    scratch = []
    if nparts > 1:
        scratch += [pltpu.VMEM((skv, nparts * HEAD_DIM), BF16), pltpu.VMEM((HEAD_DIM, tq), F32)]
    scratch += [pltpu.VMEM((nkv, dv + ONES_ROWS, tk), BF16), pltpu.VMEM((2, tk, tq), F32),
                pltpu.VMEM((n_sub, 1, tq), F32), pltpu.VMEM((n_sub, dv + ONES_ROWS, tq), F32)]
    kern = functools.partial(_attn_t_kernel, nparts=nparts, tk=tk, nkv=nkv, n_sub=n_sub)
    return pl.pallas_call(
        kern,
        grid=(b, n_heads, s // tqb),
        in_specs=in_specs,
        out_specs=pl.BlockSpec((None, tqb, dv), lambda bi, h, qi: (bi, qi, h)),
        out_shape=jax.ShapeDtypeStruct((b, s, n_heads * dv), BF16),
        scratch_shapes=scratch,
        compiler_params=_params(),
        name="attention_t",
    )(*qs, *gs, cos_t, sin_t, *ks, vt)


def _mm_vt_kernel(*refs, scaled):
    a_ref, wt_ref, o_ref = refs[0], refs[1], refs[-1]
    out = lax.dot_general(wt_ref[...], a_ref[...], (((1,), (1,)), ((), ())), preferred_element_type=F32)
    if scaled:
        out = out * refs[2][...].T[:1, :]
    o_ref[...] = out.astype(o_ref.dtype)


def mm_vt(a, wt, tk, tn, r=None):
    m, k = a.shape
    cols = wt.shape[0]
    tr = min(tk, 512)
    per = tk // tr
    in_specs = [pl.BlockSpec((tr, k), lambda i, j: (i, 0)),
                pl.BlockSpec((tn, k), lambda i, j: (j, 0))]
    args = [a, wt]
    if r is not None:
        in_specs.append(_r_spec(tr))
        args.append(r)
    return pl.pallas_call(
        functools.partial(_mm_vt_kernel, scaled=r is not None),
        grid=(m // tr, cols // tn),
        in_specs=in_specs,
        out_specs=pl.BlockSpec((None, tn, tr), lambda i, j: (i // per, j, i % per)),
        out_shape=jax.ShapeDtypeStruct((m // tk, cols, tk), BF16),
        compiler_params=_params(),
        name="mm_vt",
    )(*args)


def _deinterleave(w, width):
    lead = w.shape[:-1]
    n = w.shape[-1] // width
    w = w.reshape(lead + (n, width // 2, 2))
    w = jnp.swapaxes(w, -1, -2)
    return w.reshape(lead + (n * width,))


def _pad_groups(w, width, to):
    lead = w.shape[:-1]
    n = w.shape[-1] // width
    w = w.reshape(lead + (n, width))
    w = jnp.pad(w, [(0, 0)] * len(lead) + [(0, 0), (0, to - width)])
    return w.reshape(lead + (n * to,))


def _rope_angles(seq_len, rot_dim):
    rows = seq_len // GRID_W
    row, col = jnp.meshgrid(jnp.arange(rows), jnp.arange(GRID_W), indexing="ij")
    row = row.reshape(-1).astype(F32)
    col = col.reshape(-1).astype(F32)
    axis_dim = rot_dim // 2
    inv_freq = ROPE_THETA ** (-jnp.arange(0, axis_dim, 2, dtype=F32) / axis_dim)
    ang = jnp.concatenate([row[:, None] * inv_freq, col[:, None] * inv_freq], axis=-1)
    return jnp.cos(ang), jnp.sin(ang)


def _rope_tables(seq_len, rot_dim, width):
    cos, sin = _rope_angles(seq_len, rot_dim)
    half = rot_dim // 2
    z = jnp.zeros((seq_len, width - rot_dim), F32)
    zh = jnp.zeros_like(sin)
    c = jnp.concatenate([cos, cos, z], axis=-1)
    if rot_dim == width:
        return (half,), [c, jnp.concatenate([-sin, sin], axis=-1)]
    s_a = jnp.concatenate([-sin, zh, z], axis=-1)
    s_b = jnp.concatenate([zh, sin, z], axis=-1)
    return (width - half, half), [c, s_a, s_b]


def kernel(x, mem, ffn1_norm, ffn1_w_gu, ffn1_w_down, mix_norm, w_o, mem_norm, w_mem_kv, mem_q_norm, mem_k_norm, ffn2_norm, ffn2_w_gu, ffn2_w_down, a_w_in, a_q_a_norm, a_kv_a_norm, a_w_q_b, a_w_kv_b, a_q_nope_norm, a_q_pe_norm, a_k_nope_norm, a_k_pe_norm, b_w_in, b_q_norm, b_k_norm):
    b, s, d = x.shape
    n = b * s
    m_tok = mem.shape[1]
    depth = ffn1_norm.shape[0]
    hd = HEAD_DIM
    mem_w = w_mem_kv.shape[2] // 2
    mem_hd = mem_w // MEM_HEADS
    tok_w = w_o.shape[1] - mem_w
    n_heads = tok_w // hd
    q_lora = a_q_a_norm.shape[1]
    kv_lora = a_kv_a_norm.shape[1]
    n_kv = n_heads // B_GROUP

    tm = min(1024, s)
    seq_tiles = s // tm
    tq = min(1024, s)
    tk = min(2048, s // 4)
    n_sub = 2 if s // tq >= 2 else 1
    tm_mem = min(512, b * m_tok)
    tm_norm = min(256, s)

    def tile(width, pref):
        t = min(pref, width)
        while width % t:
            t -= hd
        return t

    rope_a = _rope_tables(s, A_ROPE, hd)
    rope_b = _rope_tables(s, hd, hd)
    cos_at, sin_at = (t.T for t in _rope_angles(s, A_ROPE))
    cos_bt, sin_bt = (t.T for t in _rope_angles(s, hd))
    lane_bcast = lambda g: jnp.broadcast_to(g[:, None], (g.shape[0], LANES))
    scale_a = float((hd + A_ROPE) ** -0.5) * LOG2E
    scale_b = float(hd ** -0.5) * LOG2E
    scale_m = float(mem_hd ** -0.5)

    xf = x.reshape(n, d)
    memf = mem.reshape(b * m_tok, d)
    w_gu1, w_gu2 = ffn1_w_gu.astype(BF16), ffn2_w_gu.astype(BF16)
    w_dn1, w_dn2 = ffn1_w_down.astype(BF16), ffn2_w_down.astype(BF16)
    w_ob = w_o.astype(BF16)
    d_ff = w_gu1.shape[2] // 2
    r3 = lambda t: t.reshape(b, s, t.shape[-1])

    def produce(a_list, w, scale, g_next):
        x_new, xg, ss = mm_residual(a_list, w, i, xf, scale, g_next, tm, tile(d, 512))
        return x_new, xg, row_scale(ss, d, tm)

    xg, r = scale_prep(xf, ffn1_norm[0], tm_norm)
    for i in range(depth):
        act = mm_swiglu(xg, r, w_gu1, i, tm, tile(d_ff, 512))
        xf, xg, r = produce([act], w_dn1, 0.5, mix_norm[i])
        proj = functools.partial(mm_headnorm, r=r)
        j = i // N_MIXERS
        if i % N_MIXERS == 0:
            w_in = a_w_in[j]
            o1, o2, o3 = q_lora, q_lora + kv_lora, q_lora + kv_lora + A_ROPE
            w_cq = w_in[:, :o1].astype(BF16)
            w_ckv = w_in[:, o1:o2].astype(BF16)
            w_kpe = _pad_groups(_deinterleave(w_in[:, o2:o3], A_ROPE), A_ROPE, hd).astype(BF16)
            w_qm = w_in[:, o3:].astype(BF16)
            wq = a_w_q_b[j].reshape(q_lora, n_heads, hd + A_ROPE)
            w_qn = wq[:, :, :hd].reshape(q_lora, n_heads * hd).astype(BF16)
            w_qp = _deinterleave(wq[:, :, hd:].reshape(q_lora, n_heads * A_ROPE), A_ROPE).astype(BF16)
            wkv = a_w_kv_b[j].reshape(kv_lora, n_heads, 2 * hd)
            w_kn = wkv[:, :, :hd].reshape(kv_lora, n_heads * hd).astype(BF16)
            w_v = wkv[:, :, hd:].reshape(kv_lora, n_heads * hd).astype(BF16)
            g_qpe = _deinterleave(a_q_pe_norm[j], A_ROPE) * scale_a
            g_kpe = _pad_groups(_deinterleave(a_k_pe_norm[j], A_ROPE), A_ROPE, hd)

            cqn = proj(xg, w_cq, a_q_a_norm[j], q_lora, q_lora, tm, q_lora)
            ckvn = proj(xg, w_ckv, a_kv_a_norm[j], kv_lora, kv_lora, tm, kv_lora)
            kpe = proj(xg, w_kpe, g_kpe, hd, A_ROPE, tm, hd, rope=rope_a, seq_tiles=seq_tiles)
            mq = proj(xg, w_qm, mem_q_norm[i] * scale_m, mem_hd, mem_hd, tm, tile(mem_w, 1024))
            tw = tile(n_heads * hd, 1024)
            qn = mm_cast(cqn, w_qn, tm, tw, F32)
            qp = mm_cast(cqn, w_qp, tm, tile(n_heads * A_ROPE, 1024), F32)
            kn = mm_headnorm(ckvn, w_kn, a_k_nope_norm[j], hd, hd, tm, tw)
            vt = mm_vt(ckvn, w_v.T, tk, tw).reshape(b, s // tk, n_heads * hd, tk)
            tok = attention_t([r3(qn), r3(qp)], [lane_bcast(a_q_nope_norm[j] * scale_a), lane_bcast(g_qpe)],
                              cos_at, sin_at, [r3(kn), r3(kpe)], vt, n_heads, 1, tq, n_sub)
        else:
            w_in = b_w_in[j]
            qw, kw = n_heads * hd, n_kv * hd
            w_q = _deinterleave(w_in[:, :qw], hd).astype(BF16)
            w_k = _deinterleave(w_in[:, qw:qw + kw], hd).astype(BF16)
            w_v = w_in[:, qw + kw:qw + 2 * kw].astype(BF16)
            w_qm = w_in[:, qw + 2 * kw:].astype(BF16)
            g_q = _deinterleave(b_q_norm[j], hd) * scale_b
            g_k = _deinterleave(b_k_norm[j], hd)
            q = mm_cast(xg, w_q, tm, tile(qw, 1024), F32, r=r)
            k = proj(xg, w_k, g_k, hd, hd, tm, tile(kw, 1024), rope=rope_b, seq_tiles=seq_tiles)
            vt = mm_vt(xg, w_v.T, tk, tile(kw, 1024), r=r).reshape(b, s // tk, kw, tk)
            mq = proj(xg, w_qm, mem_q_norm[i] * scale_m, mem_hd, mem_hd, tm, tile(mem_w, 1024))
            tok = attention_t([r3(q)], [lane_bcast(g_q)], cos_bt, sin_bt, [r3(k)], vt, n_heads, B_GROUP, tq, n_sub)

        memn = rmsnorm(memf, mem_norm[i], tm_mem)
        w_mkv = w_mem_kv[i].astype(BF16)
        mk = mm_headnorm(memn, w_mkv[:, :mem_w], mem_k_norm[i], mem_hd, mem_hd, tm_mem, tile(mem_w, 1024))
        mv = mm_cast(memn, w_mkv[:, mem_w:], tm_mem, tile(mem_w, 1024))
        mo = attention([mq.reshape(b, s, mem_w)], [mk.reshape(b, m_tok, mem_w)], mv.reshape(b, m_tok, mem_w),
                       MEM_HEADS, 1, mem_hd, mem_hd, tq, m_tok)
        xf, xg, r = produce([tok.reshape(n, tok_w), mo.reshape(n, mem_w)], w_ob, 1.0, ffn2_norm[i])
        act = mm_swiglu(xg, r, w_gu2, i, tm, tile(d_ff, 512))
        if i + 1 < depth:
            xf, xg, r = produce([act], w_dn2, 0.5, ffn1_norm[i + 1])
        else:
            xf = mm_residual([act], w_dn2, i, xf, 0.5, None, tm, tile(d, 512))
    return xf.reshape(b, s, d)
```

```python
import functools

import jax
import jax.numpy as jnp
from jax import lax
from jax.experimental import pallas as pl
from jax.experimental.pallas import tpu as pltpu

EPS = 1e-6
GRID_W = 64
ROPE_THETA = 10000.0
LANES = 128
HEAD_DIM = LANES
A_ROPE = 64
MEM_HEADS = 4
LOG2E = 1.4426950408889634
N_MIXERS = 2
B_GROUP = 4
ONES_ROWS = 16
MAX_STATIC_UNITS = 8
V7X_VMEM_BYTES = 64 * 1024 * 1024
VMEM_LIMIT = V7X_VMEM_BYTES * 7 // 8

F32 = jnp.float32
BF16 = jnp.bfloat16


def _params():
    return pltpu.CompilerParams(vmem_limit_bytes=VMEM_LIMIT)


def _rmsnorm_kernel(x_ref, g_ref, o_ref):
    x = x_ref[...]
    ms = jnp.mean(x * x, axis=-1, keepdims=True)
    o_ref[...] = (x * lax.rsqrt(ms + EPS) * g_ref[...]).astype(o_ref.dtype)


def rmsnorm(x, g, tm):
    m, d = x.shape
    return pl.pallas_call(
        _rmsnorm_kernel,
        grid=(m // tm,),
        in_specs=[pl.BlockSpec((tm, d), lambda i: (i, 0)),
                  pl.BlockSpec((1, d), lambda i: (0, 0))],
        out_specs=pl.BlockSpec((tm, d), lambda i: (i, 0)),
        out_shape=jax.ShapeDtypeStruct((m, d), BF16),
        compiler_params=_params(),
        name="rmsnorm",
    )(x, g.reshape(1, d))


def _row_scale_kernel(ss_ref, r_ref, *, inv_d):
    ss = ss_ref[...]
    tot = ss[:, :LANES]
    for c in range(1, ss.shape[1] // LANES):
        tot = tot + ss[:, c * LANES:(c + 1) * LANES]
    r_ref[...] = lax.rsqrt(tot * inv_d + EPS)


def row_scale(ss, d_norm, tm):
    m, w = ss.shape
    return pl.pallas_call(
        functools.partial(_row_scale_kernel, inv_d=1.0 / d_norm),
        grid=(m // tm,),
        in_specs=[pl.BlockSpec((tm, w), lambda i: (i, 0))],
        out_specs=pl.BlockSpec((tm, LANES), lambda i: (i, 0)),
        out_shape=jax.ShapeDtypeStruct((m, LANES), F32),
        compiler_params=_params(),
        name="row_scale",
    )(ss)


def _widen(r, width):
    return r if width == LANES else jnp.tile(r, (1, width // LANES))


def _scale_prep_kernel(x_ref, g_ref, xg_ref, r_ref):
    x = x_ref[...]
    xg_ref[...] = (x * g_ref[...]).astype(BF16)
    r = lax.rsqrt(jnp.mean(x * x, axis=-1, keepdims=True) + EPS)
    r_ref[...] = jnp.broadcast_to(r, r_ref.shape)


def scale_prep(x, g, tm):
    m, d = x.shape
    return pl.pallas_call(
        _scale_prep_kernel,
        grid=(m // tm,),
        in_specs=[pl.BlockSpec((tm, d), lambda i: (i, 0)),
                  pl.BlockSpec((1, d), lambda i: (0, 0))],
        out_specs=[pl.BlockSpec((tm, d), lambda i: (i, 0)),
                   pl.BlockSpec((tm, LANES), lambda i: (i, 0))],
        out_shape=[jax.ShapeDtypeStruct((m, d), BF16), jax.ShapeDtypeStruct((m, LANES), F32)],
        compiler_params=_params(),
        name="scale_prep",
    )(x, g.reshape(1, d))


def _mm_cast_kernel(*refs, scaled):
    a_ref, b_ref, o_ref = refs[0], refs[1], refs[-1]
    acc = jnp.dot(a_ref[...], b_ref[...], preferred_element_type=F32)
    if scaled:
        acc = acc * _widen(refs[2][...], acc.shape[1])
    o_ref[...] = acc.astype(o_ref.dtype)


def _mm_swiglu_kernel(a_ref, r_ref, bg_ref, bu_ref, o_ref):
    a = a_ref[...]
    r = _widen(r_ref[...], o_ref.shape[1])
    g = jnp.dot(a, bg_ref[...], preferred_element_type=F32) * r
    u = jnp.dot(a, bu_ref[...], preferred_element_type=F32) * r
    o_ref[...] = (g * (1.0 / (1.0 + jnp.exp(-g))) * u).astype(o_ref.dtype)


def _mm_residual_kernel(*refs, n_a, scale, emit_next):
    a_refs = refs[:n_a]
    b_refs = refs[n_a:2 * n_a]
    r_ref = refs[2 * n_a]
    acc = jnp.dot(a_refs[0][...], b_refs[0][...], preferred_element_type=F32)
    for a_ref, b_ref in zip(a_refs[1:], b_refs[1:]):
        acc = acc + jnp.dot(a_ref[...], b_ref[...], preferred_element_type=F32)
    x = r_ref[...] + (acc if scale == 1.0 else scale * acc)
    if emit_next:
        g_ref = refs[2 * n_a + 1]
        o_ref, xg_ref, ss_ref = refs[-3:]
        xg_ref[...] = (x * g_ref[...]).astype(BF16)
        ss_ref[...] = jnp.broadcast_to(jnp.sum(x * x, axis=-1, keepdims=True), ss_ref.shape)
    else:
        o_ref = refs[-1]
    o_ref[...] = x


def _mm_headnorm_kernel(*refs, cw, inv_count, shifts, scaled):
    a_ref, b_ref = refs[0], refs[1]
    nxt = 2
    acc = jnp.dot(a_ref[...], b_ref[...], preferred_element_type=F32)
    if scaled:
        acc = acc * _widen(refs[2][...], acc.shape[1])
        nxt = 3
    g = refs[nxt][...]
    tabs = refs[nxt + 1:-1]
    o_ref = refs[-1]
    for c in range(acc.shape[1] // cw):
        x = acc[:, c * cw:(c + 1) * cw]
        ms = jnp.sum(x * x, axis=-1, keepdims=True) * inv_count
        y = x * lax.rsqrt(ms + EPS) * g
        if shifts:
            out = y * tabs[0][...]
            for k, sh in enumerate(shifts):
                out = out + pltpu.roll(y, sh, 1) * tabs[1 + k][...]
            y = out
        o_ref[:, c * cw:(c + 1) * cw] = y.astype(o_ref.dtype)


def _r_spec(tm):
    return pl.BlockSpec((tm, LANES), lambda i, j: (i, 0))


def mm_cast(a, b, tm, tn, out_dtype=BF16, r=None):
    m, k = a.shape
    n_out = b.shape[1]
    in_specs = [pl.BlockSpec((tm, k), lambda i, j: (i, 0)),
                pl.BlockSpec((k, tn), lambda i, j: (0, j))]
    args = [a, b]
    if r is not None:
        in_specs.append(_r_spec(tm))
        args.append(r)
    return pl.pallas_call(
        functools.partial(_mm_cast_kernel, scaled=r is not None),
        grid=(m // tm, n_out // tn),
        in_specs=in_specs,
        out_specs=pl.BlockSpec((tm, tn), lambda i, j: (i, j)),
        out_shape=jax.ShapeDtypeStruct((m, n_out), out_dtype),
        compiler_params=_params(),
        name="mm_cast",
    )(*args)


def mm_swiglu(xg, r, w_gu, layer, tm, tn):
    m, k = xg.shape
    d_ff = w_gu.shape[2] // 2
    off = d_ff // tn
    return pl.pallas_call(
        _mm_swiglu_kernel,
        grid=(m // tm, d_ff // tn),
        in_specs=[pl.BlockSpec((tm, k), lambda i, j: (i, 0)),
                  _r_spec(tm),
                  pl.BlockSpec((None, k, tn), lambda i, j: (layer, 0, j)),
                  pl.BlockSpec((None, k, tn), lambda i, j: (layer, 0, j + off))],
        out_specs=pl.BlockSpec((tm, tn), lambda i, j: (i, j)),
        out_shape=jax.ShapeDtypeStruct((m, d_ff), BF16),
        compiler_params=_params(),
        name="mm_swiglu",
    )(xg, r, w_gu, w_gu)


def mm_residual(a_list, w, layer, res, scale, g_next, tm, tn):
    m = a_list[0].shape[0]
    n_out = w.shape[2]
    in_specs = [pl.BlockSpec((tm, a.shape[1]), lambda i, j: (i, 0)) for a in a_list]
    row = 0
    for a in a_list:
        ka = a.shape[1]
        assert row % ka == 0
        in_specs.append(pl.BlockSpec((None, ka, tn), functools.partial(lambda i, j, rb: (layer, rb, j), rb=row // ka)))
        row += ka
    assert row == w.shape[1]
    tile_spec = pl.BlockSpec((tm, tn), lambda i, j: (i, j))
    in_specs.append(tile_spec)
    args = list(a_list) + [w] * len(a_list) + [res]
    emit_next = g_next is not None
    if emit_next:
        in_specs.append(pl.BlockSpec((1, tn), lambda i, j: (0, j)))
        args.append(g_next.reshape(1, n_out))
        out_specs = [tile_spec, tile_spec, pl.BlockSpec((tm, LANES), lambda i, j: (i, j))]
        out_shape = [jax.ShapeDtypeStruct((m, n_out), F32), jax.ShapeDtypeStruct((m, n_out), BF16),
                     jax.ShapeDtypeStruct((m, (n_out // tn) * LANES), F32)]
    else:
        out_specs = tile_spec
        out_shape = jax.ShapeDtypeStruct((m, n_out), F32)
    return pl.pallas_call(
        functools.partial(_mm_residual_kernel, n_a=len(a_list), scale=scale, emit_next=emit_next),
        grid=(m // tm, n_out // tn),
        in_specs=in_specs,
        out_specs=out_specs,
        out_shape=out_shape,
        compiler_params=_params(),
        name="mm_residual",
    )(*args)


def mm_headnorm(a, b, g, cw, count, tm, tn, rope=None, seq_tiles=1, r=None):
    m, k = a.shape
    n_out = b.shape[1]
    in_specs = [pl.BlockSpec((tm, k), lambda i, j: (i, 0)),
                pl.BlockSpec((k, tn), lambda i, j: (0, j))]
    args = [a, b]
    if r is not None:
        in_specs.append(_r_spec(tm))
        args.append(r)
    in_specs.append(pl.BlockSpec((1, cw), lambda i, j: (0, 0)))
    args.append(g.reshape(1, cw))
    shifts = ()
    if rope is not None:
        shifts, tabs = rope
        for t in tabs:
            in_specs.append(pl.BlockSpec((tm, cw), lambda i, j: (i % seq_tiles, 0)))
            args.append(t)
    kern = functools.partial(_mm_headnorm_kernel, cw=cw, inv_count=1.0 / count, shifts=tuple(shifts),
                             scaled=r is not None)
    return pl.pallas_call(
        kern,
        grid=(m // tm, n_out // tn),
        in_specs=in_specs,
        out_specs=pl.BlockSpec((tm, tn), lambda i, j: (i, j)),
        out_shape=jax.ShapeDtypeStruct((m, n_out), BF16),
        compiler_params=_params(),
        name="mm_headnorm",
    )(*args)


def _attn_kernel(*refs, nparts, tk, nkv):
    q_refs = refs[:nparts]
    k_refs = refs[nparts:2 * nparts]
    v_ref = refs[2 * nparts]
    o_ref = refs[2 * nparts + 1]
    if nparts > 1:
        q = jnp.concatenate([r[...] for r in q_refs], axis=-1)
    else:
        q = q_refs[0][...]
    tq = q.shape[0]
    dv = v_ref.shape[-1]

    def body(t, carry):
        m, l, acc = carry
        off = pl.multiple_of(t * tk, tk)
        if nparts > 1:
            k = jnp.concatenate([r[pl.ds(off, tk), :] for r in k_refs], axis=-1)
        else:
            k = k_refs[0][pl.ds(off, tk), :]
        s = lax.dot_general(q, k, (((1,), (1,)), ((), ())), preferred_element_type=F32)
        m_new = jnp.maximum(m, jnp.max(s, axis=-1, keepdims=True))
        p = jnp.exp(s - m_new)
        alpha = jnp.exp(m - m_new)
        l = alpha * l + jnp.sum(p, axis=-1, keepdims=True)
        acc = alpha * acc + jnp.dot(p.astype(BF16), v_ref[pl.ds(off, tk), :], preferred_element_type=F32)
        return m_new, l, acc

    init = (jnp.full((tq, 1), -jnp.inf, F32), jnp.zeros((tq, 1), F32), jnp.zeros((tq, dv), F32))
    if nkv == 1:
        m, l, acc = body(0, init)
    else:
        m, l, acc = lax.fori_loop(0, nkv, body, init)
    o_ref[...] = (acc / l).astype(o_ref.dtype)


def attention(qs, ks, v, n_heads, group, dq, dv, tq, tk):
    b, s, _ = qs[0].shape
    skv = v.shape[1]
    nparts = len(qs)
    in_specs = [pl.BlockSpec((None, tq, dq), lambda bi, h, qi: (bi, qi, h)) for _ in qs]
    for k in ks:
        if k.shape[2] == dq:
            in_specs.append(pl.BlockSpec((None, skv, dq), lambda bi, h, qi: (bi, 0, 0)))
        else:
            in_specs.append(pl.BlockSpec((None, skv, dq), lambda bi, h, qi: (bi, 0, h // group)))
    in_specs.append(pl.BlockSpec((None, skv, dv), lambda bi, h, qi: (bi, 0, h // group)))
    kern = functools.partial(_attn_kernel, nparts=nparts, tk=tk, nkv=skv // tk)
    return pl.pallas_call(
        kern,
        grid=(b, n_heads, s // tq),
        in_specs=in_specs,
        out_specs=pl.BlockSpec((None, tq, dv), lambda bi, h, qi: (bi, qi, h)),
        out_shape=jax.ShapeDtypeStruct((b, s, n_heads * dv), BF16),
        compiler_params=_params(),
        name="attention",
    )(*qs, *ks, v)


def _norm_rows(x, g_ref):
    y = x * lax.rsqrt(jnp.mean(x * x, axis=0, keepdims=True) + EPS)
    return y * jnp.tile(g_ref[...], (1, x.shape[1] // LANES))


def _rope_rows(y, c, s):
    half = y.shape[0] // 2
    x0, x1 = y[:half], y[half:]
    return x0 * c - x1 * s, x0 * s + x1 * c


def _attn_t_kernel(*refs, nparts, tk, nkv, n_sub):
    q_refs = refs[:nparts]
    g_refs = refs[nparts:2 * nparts]
    cos_ref, sin_ref = refs[2 * nparts], refs[2 * nparts + 1]
    k_refs = refs[2 * nparts + 2:3 * nparts + 2]
    vt_ref = refs[3 * nparts + 2]
    o_ref = refs[3 * nparts + 3]
    scratch = refs[3 * nparts + 4:]
    dv = vt_ref.shape[1]
    tq = o_ref.shape[0] // n_sub
    first_q_tile = pl.program_id(2) == 0
    if nparts > 1:
        kcat_ref, qp_ref, vx_ref, s_ref, m_ref, acc_ref = scratch

        @pl.when(first_q_tile)
        def _():
            for i, r in enumerate(k_refs):
                kcat_ref[:, i * HEAD_DIM:(i + 1) * HEAD_DIM] = r[...]
        k_ref = kcat_ref
    else:
        vx_ref, s_ref, m_ref, acc_ref = scratch
        k_ref = k_refs[0]

    @pl.when(first_q_tile)
    def _():
        vx_ref[:, :dv, :] = vt_ref[...]
        vx_ref[:, dv:, :] = jnp.ones((nkv, ONES_ROWS, tk), BF16)

    def q_t(t):
        rows = slice(t * tq, (t + 1) * tq)
        c, sn = cos_ref[:, rows], sin_ref[:, rows]
        if nparts == 1:
            out = _rope_rows(_norm_rows(q_refs[0][rows, :].T, g_refs[0]), c, sn)
        else:
            qp_ref[...] = q_refs[1][rows, :].T
            off = pl.multiple_of((pl.program_id(1) % 2) * A_ROPE, A_ROPE)
            o0, o1 = _rope_rows(_norm_rows(qp_ref[pl.ds(off, A_ROPE), :], g_refs[1]), c, sn)
            out = (_norm_rows(q_refs[0][rows, :].T, g_refs[0]), o0, o1, jnp.zeros((A_ROPE, tq), F32))
        return jnp.concatenate(out, axis=0).astype(BF16)

    qts = [q_t(t) for t in range(n_sub)]

    def qk(t, c, slot):
        off = pl.multiple_of(c * tk, tk)
        s_ref[slot] = jnp.dot(k_ref[pl.ds(off, tk), :], qts[t], preferred_element_type=F32)

    def softmax_pv(t, c, slot):
        s = s_ref[slot]
        m_old = m_ref[t]
        m_new = jnp.maximum(m_old, jnp.max(s, axis=0, keepdims=True))
        alpha = jnp.exp2(m_old - m_new)
        p = jnp.exp2(s - m_new)
        m_ref[t] = m_new
        acc_ref[t] = alpha * acc_ref[t] + jnp.dot(vx_ref[c], p.astype(BF16), preferred_element_type=F32)

    def finish(t):
        out = acc_ref[t, :dv, :] * (1.0 / acc_ref[t, dv:dv + 1, :])
        o_ref[t * tq:(t + 1) * tq, :] = out.T.astype(o_ref.dtype)

    m_ref[...] = jnp.full(m_ref.shape, -jnp.inf, F32)
    acc_ref[...] = jnp.zeros(acc_ref.shape, F32)

    if n_sub * nkv <= MAX_STATIC_UNITS:
        units = [(t, c) for t in range(n_sub) for c in range(nkv)]
        qk(*units[0], 0)
        for u, (t, c) in enumerate(units):
            if u + 1 < len(units):
                qk(*units[u + 1], (u + 1) % 2)
            softmax_pv(t, c, u % 2)
            if c == nkv - 1:
                finish(t)
    else:
        assert n_sub == 1
        qk(0, 0, 0)

        def pair(i, carry):
            c = 2 * i
            qk(0, c + 1, 1)
            softmax_pv(0, c, 0)
            qk(0, c + 2, 0)
            softmax_pv(0, c + 1, 1)
            return carry

        lax.fori_loop(0, nkv // 2 - 1, pair, 0)
        qk(0, nkv - 1, 1)
        softmax_pv(0, nkv - 2, 0)
        softmax_pv(0, nkv - 1, 1)
        finish(0)


def attention_t(qs, gs, cos_t, sin_t, ks, vt, n_heads, group, tq, n_sub):
    b, s, _ = qs[0].shape
    skv = ks[0].shape[1]
    nkv, tk = vt.shape[1], vt.shape[3]
    n_kv = n_heads // group
    dv = vt.shape[2] // n_kv
    nparts = len(qs)
    tqb = tq * n_sub
    assert nkv % 2 == 0 and nkv * tk == skv and s % tqb == 0
    in_specs = [pl.BlockSpec((None, tqb, HEAD_DIM), lambda bi, h, qi: (bi, qi, h))]
    if nparts > 1:
        in_specs.append(pl.BlockSpec((None, tqb, HEAD_DIM), lambda bi, h, qi: (bi, qi, h // 2)))
    in_specs += [pl.BlockSpec(g.shape, lambda bi, h, qi: (0, 0)) for g in gs]
    in_specs += [pl.BlockSpec((cos_t.shape[0], tqb), lambda bi, h, qi: (0, qi))] * 2
    for k in ks:
        if k.shape[2] == n_kv * HEAD_DIM:
            in_specs.append(pl.BlockSpec((None, skv, HEAD_DIM), lambda bi, h, qi: (bi, 0, h // group)))
        else:
            in_specs.append(pl.BlockSpec((None, skv, HEAD_DIM), lambda bi, h, qi: (bi, 0, 0)))
    in_specs.append(pl.BlockSpec((None, nkv, dv, tk), lambda bi, h, qi: (bi, 0, h // group, 0)))
    scratch = []
    if nparts > 1:
        scratch += [pltpu.VMEM((skv, nparts * HEAD_DIM), BF16), pltpu.VMEM((HEAD_DIM, tq), F32)]
    scratch += [pltpu.VMEM((nkv, dv + ONES_ROWS, tk), BF16), pltpu.VMEM((2, tk, tq), F32),
                pltpu.VMEM((n_sub, 1, tq), F32), pltpu.VMEM((n_sub, dv + ONES_ROWS, tq), F32)]
    kern = functools.partial(_attn_t_kernel, nparts=nparts, tk=tk, nkv=nkv, n_sub=n_sub)
    return pl.pallas_call(
        kern,
        grid=(b, n_heads, s // tqb),
        in_specs=in_specs,
        out_specs=pl.BlockSpec((None, tqb, dv), lambda bi, h, qi: (bi, qi, h)),
        out_shape=jax.ShapeDtypeStruct((b, s, n_heads * dv), BF16),
        scratch_shapes=scratch,
        compiler_params=_params(),
        name="attention_t",
    )(*qs, *gs, cos_t, sin_t, *ks, vt)


def _mm_vt_kernel(*refs, scaled):
    a_ref, wt_ref, o_ref = refs[0], refs[1], refs[-1]
    out = lax.dot_general(wt_ref[...], a_ref[...], (((1,), (1,)), ((), ())), preferred_element_type=F32)
    if scaled:
        out = out * refs[2][...].T[:1, :]
    o_ref[...] = out.astype(o_ref.dtype)


def mm_vt(a, wt, tk, tn, r=None):
    m, k = a.shape
    cols = wt.shape[0]
    tr = min(tk, 512 if k > 1024 else 2048)
    per = tk // tr
    in_specs = [pl.BlockSpec((tr, k), lambda i, j: (i, 0)),
                pl.BlockSpec((tn, k), lambda i, j: (j, 0))]
    args = [a, wt]
    if r is not None:
        in_specs.append(_r_spec(tr))
        args.append(r)
    return pl.pallas_call(
        functools.partial(_mm_vt_kernel, scaled=r is not None),
        grid=(m // tr, cols // tn),
        in_specs=in_specs,
        out_specs=pl.BlockSpec((None, tn, tr), lambda i, j: (i // per, j, i % per)),
        out_shape=jax.ShapeDtypeStruct((m // tk, cols, tk), BF16),
        compiler_params=_params(),
        name="mm_vt",
    )(*args)


def _deinterleave(w, width):
    lead = w.shape[:-1]
    n = w.shape[-1] // width
    w = w.reshape(lead + (n, width // 2, 2))
    w = jnp.swapaxes(w, -1, -2)
    return w.reshape(lead + (n * width,))


def _pad_groups(w, width, to):
    lead = w.shape[:-1]
    n = w.shape[-1] // width
    w = w.reshape(lead + (n, width))
    w = jnp.pad(w, [(0, 0)] * len(lead) + [(0, 0), (0, to - width)])
    return w.reshape(lead + (n * to,))


def _rope_angles(seq_len, rot_dim):
    rows = seq_len // GRID_W
    row, col = jnp.meshgrid(jnp.arange(rows), jnp.arange(GRID_W), indexing="ij")
    row = row.reshape(-1).astype(F32)
    col = col.reshape(-1).astype(F32)
    axis_dim = rot_dim // 2
    inv_freq = ROPE_THETA ** (-jnp.arange(0, axis_dim, 2, dtype=F32) / axis_dim)
    ang = jnp.concatenate([row[:, None] * inv_freq, col[:, None] * inv_freq], axis=-1)
    return jnp.cos(ang), jnp.sin(ang)


def _rope_tables(seq_len, rot_dim, width):
    cos, sin = _rope_angles(seq_len, rot_dim)
    half = rot_dim // 2
    z = jnp.zeros((seq_len, width - rot_dim), F32)
    zh = jnp.zeros_like(sin)
    c = jnp.concatenate([cos, cos, z], axis=-1)
    if rot_dim == width:
        return (half,), [c, jnp.concatenate([-sin, sin], axis=-1)]
    s_a = jnp.concatenate([-sin, zh, z], axis=-1)
    s_b = jnp.concatenate([zh, sin, z], axis=-1)
    return (width - half, half), [c, s_a, s_b]


def kernel(x, mem, ffn1_norm, ffn1_w_gu, ffn1_w_down, mix_norm, w_o, mem_norm, w_mem_kv, mem_q_norm, mem_k_norm, ffn2_norm, ffn2_w_gu, ffn2_w_down, a_w_in, a_q_a_norm, a_kv_a_norm, a_w_q_b, a_w_kv_b, a_q_nope_norm, a_q_pe_norm, a_k_nope_norm, a_k_pe_norm, b_w_in, b_q_norm, b_k_norm):
    b, s, d = x.shape
    n = b * s
    m_tok = mem.shape[1]
    depth = ffn1_norm.shape[0]
    hd = HEAD_DIM
    mem_w = w_mem_kv.shape[2] // 2
    mem_hd = mem_w // MEM_HEADS
    tok_w = w_o.shape[1] - mem_w
    n_heads = tok_w // hd
    q_lora = a_q_a_norm.shape[1]
    kv_lora = a_kv_a_norm.shape[1]
    n_kv = n_heads // B_GROUP

    tm = min(1024, s)
    seq_tiles = s // tm
    tq = min(1024, s)
    tk = min(2048, s // 4)
    n_sub = 2 if s // tq >= 2 else 1
    tm_mem = min(512, b * m_tok)
    tm_norm = min(256, s)

    def tile(width, pref):
        t = min(pref, width)
        while width % t:
            t -= hd
        return t

    rope_a = _rope_tables(s, A_ROPE, hd)
    rope_b = _rope_tables(s, hd, hd)
    cos_at, sin_at = (t.T for t in _rope_angles(s, A_ROPE))
    cos_bt, sin_bt = (t.T for t in _rope_angles(s, hd))
    lane_bcast = lambda g: jnp.broadcast_to(g[:, None], (g.shape[0], LANES))
    scale_a = float((hd + A_ROPE) ** -0.5) * LOG2E
    scale_b = float(hd ** -0.5) * LOG2E
    scale_m = float(mem_hd ** -0.5)

    xf = x.reshape(n, d)
    memf = mem.reshape(b * m_tok, d)
    w_gu1, w_gu2 = ffn1_w_gu.astype(BF16), ffn2_w_gu.astype(BF16)
    w_dn1, w_dn2 = ffn1_w_down.astype(BF16), ffn2_w_down.astype(BF16)
    w_ob = w_o.astype(BF16)
    d_ff = w_gu1.shape[2] // 2
    r3 = lambda t: t.reshape(b, s, t.shape[-1])

    def produce(a_list, w, scale, g_next):
        x_new, xg, ss = mm_residual(a_list, w, i, xf, scale, g_next, tm, tile(d, 512))
        return x_new, xg, row_scale(ss, d, tm)

    xg, r = scale_prep(xf, ffn1_norm[0], tm_norm)
    for i in range(depth):
        act = mm_swiglu(xg, r, w_gu1, i, tm, tile(d_ff, 768))
        xf, xg, r = produce([act], w_dn1, 0.5, mix_norm[i])
        proj = functools.partial(mm_headnorm, r=r)
        j = i // N_MIXERS
        if i % N_MIXERS == 0:
            w_in = a_w_in[j]
            o1, o2, o3 = q_lora, q_lora + kv_lora, q_lora + kv_lora + A_ROPE
            w_cq = w_in[:, :o1].astype(BF16)
            w_ckv = w_in[:, o1:o2].astype(BF16)
            w_kpe = _pad_groups(_deinterleave(w_in[:, o2:o3], A_ROPE), A_ROPE, hd).astype(BF16)
            w_qm = w_in[:, o3:].astype(BF16)
            wq = a_w_q_b[j].reshape(q_lora, n_heads, hd + A_ROPE)
            w_qn = wq[:, :, :hd].reshape(q_lora, n_heads * hd).astype(BF16)
            w_qp = _deinterleave(wq[:, :, hd:].reshape(q_lora, n_heads * A_ROPE), A_ROPE).astype(BF16)
            wkv = a_w_kv_b[j].reshape(kv_lora, n_heads, 2 * hd)
            w_kn = wkv[:, :, :hd].reshape(kv_lora, n_heads * hd).astype(BF16)
            w_v = wkv[:, :, hd:].reshape(kv_lora, n_heads * hd).astype(BF16)
            g_qpe = _deinterleave(a_q_pe_norm[j], A_ROPE) * scale_a
            g_kpe = _pad_groups(_deinterleave(a_k_pe_norm[j], A_ROPE), A_ROPE, hd)

            cqn = proj(xg, w_cq, a_q_a_norm[j], q_lora, q_lora, tm, q_lora)
            ckvn = proj(xg, w_ckv, a_kv_a_norm[j], kv_lora, kv_lora, tm, kv_lora)
            kpe = proj(xg, w_kpe, g_kpe, hd, A_ROPE, tm, hd, rope=rope_a, seq_tiles=seq_tiles)
            mq = proj(xg, w_qm, mem_q_norm[i] * scale_m, mem_hd, mem_hd, tm, tile(mem_w, 1024))
            tw = tile(n_heads * hd, 1024)
            qn = mm_cast(cqn, w_qn, tm, tw, F32)
            qp = mm_cast(cqn, w_qp, tm, tile(n_heads * A_ROPE, 1024), F32)
            kn = mm_headnorm(ckvn, w_kn, a_k_nope_norm[j], hd, hd, tm, tw)
            vt = mm_vt(ckvn, w_v.T, tk, tw).reshape(b, s // tk, n_heads * hd, tk)
            tok = attention_t([r3(qn), r3(qp)], [lane_bcast(a_q_nope_norm[j] * scale_a), lane_bcast(g_qpe)],
                              cos_at, sin_at, [r3(kn), r3(kpe)], vt, n_heads, 1, tq, n_sub)
        else:
            w_in = b_w_in[j]
            qw, kw = n_heads * hd, n_kv * hd
            w_q = _deinterleave(w_in[:, :qw], hd).astype(BF16)
            w_k = _deinterleave(w_in[:, qw:qw + kw], hd).astype(BF16)
            w_v = w_in[:, qw + kw:qw + 2 * kw].astype(BF16)
            w_qm = w_in[:, qw + 2 * kw:].astype(BF16)
            g_q = _deinterleave(b_q_norm[j], hd) * scale_b
            g_k = _deinterleave(b_k_norm[j], hd)
            q = mm_cast(xg, w_q, tm, tile(qw, 1024), F32, r=r)
            k = proj(xg, w_k, g_k, hd, hd, tm, tile(kw, 1024), rope=rope_b, seq_tiles=seq_tiles)
            vt = mm_vt(xg, w_v.T, tk, tile(kw, 1024), r=r).reshape(b, s // tk, kw, tk)
            mq = proj(xg, w_qm, mem_q_norm[i] * scale_m, mem_hd, mem_hd, tm, tile(mem_w, 1024))
            tok = attention_t([r3(q)], [lane_bcast(g_q)], cos_bt, sin_bt, [r3(k)], vt, n_heads, B_GROUP, tq, n_sub)

        memn = rmsnorm(memf, mem_norm[i], tm_mem)
        w_mkv = w_mem_kv[i].astype(BF16)
        mk = mm_headnorm(memn, w_mkv[:, :mem_w], mem_k_norm[i], mem_hd, mem_hd, tm_mem, tile(mem_w, 1024))
        mv = mm_cast(memn, w_mkv[:, mem_w:], tm_mem, tile(mem_w, 1024))
        mo = attention([mq.reshape(b, s, mem_w)], [mk.reshape(b, m_tok, mem_w)], mv.reshape(b, m_tok, mem_w),
                       MEM_HEADS, 1, mem_hd, mem_hd, tq, m_tok)
        xf, xg, r = produce([tok.reshape(n, tok_w), mo.reshape(n, mem_w)], w_ob, 1.0, ffn2_norm[i])
        act = mm_swiglu(xg, r, w_gu2, i, tm, tile(d_ff, 768))
        if i + 1 < depth:
            xf, xg, r = produce([act], w_dn2, 0.5, ffn1_norm[i + 1])
        else:
            xf = mm_residual([act], w_dn2, i, xf, 0.5, None, tm, tile(d, 512))
    return xf.reshape(b, s, d)
```

```python
import functools

import jax
import jax.numpy as jnp
from jax import lax
from jax.experimental import pallas as pl
from jax.experimental.pallas import tpu as pltpu

EPS = 1e-6
GRID_W = 64
ROPE_THETA = 10000.0
LANES = 128
HEAD_DIM = LANES
A_ROPE = 64
MEM_HEADS = 4
LOG2E = 1.4426950408889634
N_MIXERS = 2
B_GROUP = 4
ONES_ROWS = 16
MAX_STATIC_UNITS = 8
V7X_VMEM_BYTES = 64 * 1024 * 1024
VMEM_LIMIT = V7X_VMEM_BYTES * 7 // 8

F32 = jnp.float32
BF16 = jnp.bfloat16


def _params():
    return pltpu.CompilerParams(vmem_limit_bytes=VMEM_LIMIT)


def _rmsnorm_kernel(x_ref, g_ref, o_ref):
    x = x_ref[...]
    ms = jnp.mean(x * x, axis=-1, keepdims=True)
    o_ref[...] = (x * lax.rsqrt(ms + EPS) * g_ref[...]).astype(o_ref.dtype)


def rmsnorm(x, g, tm):
    m, d = x.shape
    return pl.pallas_call(
        _rmsnorm_kernel,
        grid=(m // tm,),
        in_specs=[pl.BlockSpec((tm, d), lambda i: (i, 0)),
                  pl.BlockSpec((1, d), lambda i: (0, 0))],
        out_specs=pl.BlockSpec((tm, d), lambda i: (i, 0)),
        out_shape=jax.ShapeDtypeStruct((m, d), BF16),
        compiler_params=_params(),
        name="rmsnorm",
    )(x, g.reshape(1, d))


def _row_scale_kernel(ss_ref, r_ref, *, inv_d):
    ss = ss_ref[...]
    tot = ss[:, :LANES]
    for c in range(1, ss.shape[1] // LANES):
        tot = tot + ss[:, c * LANES:(c + 1) * LANES]
    r_ref[...] = lax.rsqrt(tot * inv_d + EPS)


def row_scale(ss, d_norm, tm):
    m, w = ss.shape
    return pl.pallas_call(
        functools.partial(_row_scale_kernel, inv_d=1.0 / d_norm),
        grid=(m // tm,),
        in_specs=[pl.BlockSpec((tm, w), lambda i: (i, 0))],
        out_specs=pl.BlockSpec((tm, LANES), lambda i: (i, 0)),
        out_shape=jax.ShapeDtypeStruct((m, LANES), F32),
        compiler_params=_params(),
        name="row_scale",
    )(ss)


def _widen(r, width):
    return r if width == LANES else jnp.tile(r, (1, width // LANES))


def _scale_prep_kernel(x_ref, g_ref, xg_ref, r_ref):
    x = x_ref[...]
    xg_ref[...] = (x * g_ref[...]).astype(BF16)
    r = lax.rsqrt(jnp.mean(x * x, axis=-1, keepdims=True) + EPS)
    r_ref[...] = jnp.broadcast_to(r, r_ref.shape)


def scale_prep(x, g, tm):
    m, d = x.shape
    return pl.pallas_call(
        _scale_prep_kernel,
        grid=(m // tm,),
        in_specs=[pl.BlockSpec((tm, d), lambda i: (i, 0)),
                  pl.BlockSpec((1, d), lambda i: (0, 0))],
        out_specs=[pl.BlockSpec((tm, d), lambda i: (i, 0)),
                   pl.BlockSpec((tm, LANES), lambda i: (i, 0))],
        out_shape=[jax.ShapeDtypeStruct((m, d), BF16), jax.ShapeDtypeStruct((m, LANES), F32)],
        compiler_params=_params(),
        name="scale_prep",
    )(x, g.reshape(1, d))


def _mm_cast_kernel(*refs, scaled):
    a_ref, b_ref, o_ref = refs[0], refs[1], refs[-1]
    acc = jnp.dot(a_ref[...], b_ref[...], preferred_element_type=F32)
    if scaled:
        acc = acc * _widen(refs[2][...], acc.shape[1])
    o_ref[...] = acc.astype(o_ref.dtype)


def _mm_swiglu_kernel(a_ref, r_ref, bg_ref, bu_ref, o_ref):
    a = a_ref[...]
    r = _widen(r_ref[...], o_ref.shape[1])
    g = jnp.dot(a, bg_ref[...], preferred_element_type=F32) * r
    u = jnp.dot(a, bu_ref[...], preferred_element_type=F32) * r
    o_ref[...] = (g * (1.0 / (1.0 + jnp.exp(-g))) * u).astype(o_ref.dtype)


def _mm_residual_kernel(*refs, n_a, scale, emit_next):
    a_refs = refs[:n_a]
    b_refs = refs[n_a:2 * n_a]
    r_ref = refs[2 * n_a]
    acc = jnp.dot(a_refs[0][...], b_refs[0][...], preferred_element_type=F32)
    for a_ref, b_ref in zip(a_refs[1:], b_refs[1:]):
        acc = acc + jnp.dot(a_ref[...], b_ref[...], preferred_element_type=F32)
    x = r_ref[...] + (acc if scale == 1.0 else scale * acc)
    if emit_next:
        g_ref = refs[2 * n_a + 1]
        o_ref, xg_ref, ss_ref = refs[-3:]
        xg_ref[...] = (x * g_ref[...]).astype(BF16)
        ss_ref[...] = jnp.broadcast_to(jnp.sum(x * x, axis=-1, keepdims=True), ss_ref.shape)
    else:
        o_ref = refs[-1]
    o_ref[...] = x


def _mm_headnorm_kernel(*refs, cw, inv_count, shifts, scaled):
    a_ref, b_ref = refs[0], refs[1]
    nxt = 2
    acc = jnp.dot(a_ref[...], b_ref[...], preferred_element_type=F32)
    if scaled:
        acc = acc * _widen(refs[2][...], acc.shape[1])
        nxt = 3
    g = refs[nxt][...]
    tabs = refs[nxt + 1:-1]
    o_ref = refs[-1]
    for c in range(acc.shape[1] // cw):
        x = acc[:, c * cw:(c + 1) * cw]
        ms = jnp.sum(x * x, axis=-1, keepdims=True) * inv_count
        y = x * lax.rsqrt(ms + EPS) * g
        if shifts:
            out = y * tabs[0][...]
            for k, sh in enumerate(shifts):
                out = out + pltpu.roll(y, sh, 1) * tabs[1 + k][...]
            y = out
        o_ref[:, c * cw:(c + 1) * cw] = y.astype(o_ref.dtype)


def _r_spec(tm):
    return pl.BlockSpec((tm, LANES), lambda i, j: (i, 0))


def mm_cast(a, b, tm, tn, out_dtype=BF16, r=None):
    m, k = a.shape
    n_out = b.shape[1]
    in_specs = [pl.BlockSpec((tm, k), lambda i, j: (i, 0)),
                pl.BlockSpec((k, tn), lambda i, j: (0, j))]
    args = [a, b]
    if r is not None:
        in_specs.append(_r_spec(tm))
        args.append(r)
    return pl.pallas_call(
        functools.partial(_mm_cast_kernel, scaled=r is not None),
        grid=(m // tm, n_out // tn),
        in_specs=in_specs,
        out_specs=pl.BlockSpec((tm, tn), lambda i, j: (i, j)),
        out_shape=jax.ShapeDtypeStruct((m, n_out), out_dtype),
        compiler_params=_params(),
        name="mm_cast",
    )(*args)


def mm_swiglu(xg, r, w_gu, layer, tm, tn):
    m, k = xg.shape
    d_ff = w_gu.shape[2] // 2
    off = d_ff // tn
    return pl.pallas_call(
        _mm_swiglu_kernel,
        grid=(m // tm, d_ff // tn),
        in_specs=[pl.BlockSpec((tm, k), lambda i, j: (i, 0)),
                  _r_spec(tm),
                  pl.BlockSpec((None, k, tn), lambda i, j: (layer, 0, j)),
                  pl.BlockSpec((None, k, tn), lambda i, j: (layer, 0, j + off))],
        out_specs=pl.BlockSpec((tm, tn), lambda i, j: (i, j)),
        out_shape=jax.ShapeDtypeStruct((m, d_ff), BF16),
        compiler_params=_params(),
        name="mm_swiglu",
    )(xg, r, w_gu, w_gu)


def mm_residual(a_list, w, layer, res, scale, g_next, tm, tn):
    m = a_list[0].shape[0]
    n_out = w.shape[2]
    in_specs = [pl.BlockSpec((tm, a.shape[1]), lambda i, j: (i, 0)) for a in a_list]
    row = 0
    for a in a_list:
        ka = a.shape[1]
        assert row % ka == 0
        in_specs.append(pl.BlockSpec((None, ka, tn), functools.partial(lambda i, j, rb: (layer, rb, j), rb=row // ka)))
        row += ka
    assert row == w.shape[1]
    tile_spec = pl.BlockSpec((tm, tn), lambda i, j: (i, j))
    in_specs.append(tile_spec)
    args = list(a_list) + [w] * len(a_list) + [res]
    emit_next = g_next is not None
    if emit_next:
        in_specs.append(pl.BlockSpec((1, tn), lambda i, j: (0, j)))
        args.append(g_next.reshape(1, n_out))
        out_specs = [tile_spec, tile_spec, pl.BlockSpec((tm, LANES), lambda i, j: (i, j))]
        out_shape = [jax.ShapeDtypeStruct((m, n_out), F32), jax.ShapeDtypeStruct((m, n_out), BF16),
                     jax.ShapeDtypeStruct((m, (n_out // tn) * LANES), F32)]
    else:
        out_specs = tile_spec
        out_shape = jax.ShapeDtypeStruct((m, n_out), F32)
    return pl.pallas_call(
        functools.partial(_mm_residual_kernel, n_a=len(a_list), scale=scale, emit_next=emit_next),
        grid=(m // tm, n_out // tn),
        in_specs=in_specs,
        out_specs=out_specs,
        out_shape=out_shape,
        compiler_params=_params(),
        name="mm_residual",
    )(*args)


def mm_headnorm(a, b, g, cw, count, tm, tn, rope=None, seq_tiles=1, r=None):
    m, k = a.shape
    n_out = b.shape[1]
    in_specs = [pl.BlockSpec((tm, k), lambda i, j: (i, 0)),
                pl.BlockSpec((k, tn), lambda i, j: (0, j))]
    args = [a, b]
    if r is not None:
        in_specs.append(_r_spec(tm))
        args.append(r)
    in_specs.append(pl.BlockSpec((1, cw), lambda i, j: (0, 0)))
    args.append(g.reshape(1, cw))
    shifts = ()
    if rope is not None:
        shifts, tabs = rope
        for t in tabs:
            in_specs.append(pl.BlockSpec((tm, cw), lambda i, j: (i % seq_tiles, 0)))
            args.append(t)
    kern = functools.partial(_mm_headnorm_kernel, cw=cw, inv_count=1.0 / count, shifts=tuple(shifts),
                             scaled=r is not None)
    return pl.pallas_call(
        kern,
        grid=(m // tm, n_out // tn),
        in_specs=in_specs,
        out_specs=pl.BlockSpec((tm, tn), lambda i, j: (i, j)),
        out_shape=jax.ShapeDtypeStruct((m, n_out), BF16),
        compiler_params=_params(),
        name="mm_headnorm",
    )(*args)


def _attn_kernel(*refs, nparts, tk, nkv):
    q_refs = refs[:nparts]
    k_refs = refs[nparts:2 * nparts]
    v_ref = refs[2 * nparts]
    o_ref = refs[2 * nparts + 1]
    if nparts > 1:
        q = jnp.concatenate([r[...] for r in q_refs], axis=-1)
    else:
        q = q_refs[0][...]
    tq = q.shape[0]
    dv = v_ref.shape[-1]

    def body(t, carry):
        m, l, acc = carry
        off = pl.multiple_of(t * tk, tk)
        if nparts > 1:
            k = jnp.concatenate([r[pl.ds(off, tk), :] for r in k_refs], axis=-1)
        else:
            k = k_refs[0][pl.ds(off, tk), :]
        s = lax.dot_general(q, k, (((1,), (1,)), ((), ())), preferred_element_type=F32)
        m_new = jnp.maximum(m, jnp.max(s, axis=-1, keepdims=True))
        p = jnp.exp(s - m_new)
        alpha = jnp.exp(m - m_new)
        l = alpha * l + jnp.sum(p, axis=-1, keepdims=True)
        acc = alpha * acc + jnp.dot(p.astype(BF16), v_ref[pl.ds(off, tk), :], preferred_element_type=F32)
        return m_new, l, acc

    init = (jnp.full((tq, 1), -jnp.inf, F32), jnp.zeros((tq, 1), F32), jnp.zeros((tq, dv), F32))
    if nkv == 1:
        m, l, acc = body(0, init)
    else:
        m, l, acc = lax.fori_loop(0, nkv, body, init)
    o_ref[...] = (acc / l).astype(o_ref.dtype)


def attention(qs, ks, v, n_heads, group, dq, dv, tq, tk):
    b, s, _ = qs[0].shape
    skv = v.shape[1]
    nparts = len(qs)
    in_specs = [pl.BlockSpec((None, tq, dq), lambda bi, h, qi: (bi, qi, h)) for _ in qs]
    for k in ks:
        if k.shape[2] == dq:
            in_specs.append(pl.BlockSpec((None, skv, dq), lambda bi, h, qi: (bi, 0, 0)))
        else:
            in_specs.append(pl.BlockSpec((None, skv, dq), lambda bi, h, qi: (bi, 0, h // group)))
    in_specs.append(pl.BlockSpec((None, skv, dv), lambda bi, h, qi: (bi, 0, h // group)))
    kern = functools.partial(_attn_kernel, nparts=nparts, tk=tk, nkv=skv // tk)
    return pl.pallas_call(
        kern,
        grid=(b, n_heads, s // tq),
        in_specs=in_specs,
        out_specs=pl.BlockSpec((None, tq, dv), lambda bi, h, qi: (bi, qi, h)),
        out_shape=jax.ShapeDtypeStruct((b, s, n_heads * dv), BF16),
        compiler_params=_params(),
        name="attention",
    )(*qs, *ks, v)


def _norm_rows(x, g_ref):
    y = x * lax.rsqrt(jnp.mean(x * x, axis=0, keepdims=True) + EPS)
    return y * jnp.tile(g_ref[...], (1, x.shape[1] // LANES))


def _rope_rows(y, c, s):
    half = y.shape[0] // 2
    x0, x1 = y[:half], y[half:]
    return x0 * c - x1 * s, x0 * s + x1 * c


def _attn_t_kernel(*refs, nparts, tk, nkv, n_sub):
    q_refs = refs[:nparts]
    g_refs = refs[nparts:2 * nparts]
    cos_ref, sin_ref = refs[2 * nparts], refs[2 * nparts + 1]
    k_refs = refs[2 * nparts + 2:3 * nparts + 2]
    vt_ref = refs[3 * nparts + 2]
    o_ref = refs[3 * nparts + 3]
    scratch = refs[3 * nparts + 4:]
    dv = vt_ref.shape[1]
    tq = o_ref.shape[0] // n_sub
    first_q_tile = pl.program_id(2) == 0
    if nparts > 1:
        kcat_ref, qp_ref, vx_ref, s_ref, m_ref, acc_ref = scratch

        @pl.when(first_q_tile)
        def _():
            for i, r in enumerate(k_refs):
                kcat_ref[:, i * HEAD_DIM:(i + 1) * HEAD_DIM] = r[...]
        k_ref = kcat_ref
    else:
        vx_ref, s_ref, m_ref, acc_ref = scratch
        k_ref = k_refs[0]

    @pl.when(first_q_tile)
    def _():
        vx_ref[:, :dv, :] = vt_ref[...]
        vx_ref[:, dv:, :] = jnp.ones((nkv, ONES_ROWS, tk), BF16)

    def q_t(t):
        rows = slice(t * tq, (t + 1) * tq)
        c, sn = cos_ref[:, rows], sin_ref[:, rows]
        if nparts == 1:
            out = _rope_rows(_norm_rows(q_refs[0][rows, :].T, g_refs[0]), c, sn)
        else:
            qp_ref[...] = q_refs[1][rows, :].T
            off = pl.multiple_of((pl.program_id(1) % 2) * A_ROPE, A_ROPE)
            o0, o1 = _rope_rows(_norm_rows(qp_ref[pl.ds(off, A_ROPE), :], g_refs[1]), c, sn)
            out = (_norm_rows(q_refs[0][rows, :].T, g_refs[0]), o0, o1, jnp.zeros((A_ROPE, tq), F32))
        return jnp.concatenate(out, axis=0).astype(BF16)

    qts = [q_t(t) for t in range(n_sub)]

    def qk(t, c, slot):
        off = pl.multiple_of(c * tk, tk)
        s_ref[slot] = jnp.dot(k_ref[pl.ds(off, tk), :], qts[t], preferred_element_type=F32)

    def softmax_pv(t, c, slot):
        hw = tq // 2
        for h in range(2):
            cols = slice(h * hw, (h + 1) * hw)
            s = s_ref[slot, :, cols]
            m_old = m_ref[t, :, cols]
            m_new = jnp.maximum(m_old, jnp.max(s, axis=0, keepdims=True))
            alpha = jnp.exp2(m_old - m_new)
            p = jnp.exp2(s - m_new)
            m_ref[t, :, cols] = m_new
            acc_ref[t, :, cols] = alpha * acc_ref[t, :, cols] + jnp.dot(vx_ref[c], p.astype(BF16),
                                                                      preferred_element_type=F32)

    def finish(t):
        out = acc_ref[t, :dv, :] * (1.0 / acc_ref[t, dv:dv + 1, :])
        o_ref[t * tq:(t + 1) * tq, :] = out.T.astype(o_ref.dtype)

    m_ref[...] = jnp.full(m_ref.shape, -jnp.inf, F32)
    acc_ref[...] = jnp.zeros(acc_ref.shape, F32)

    if n_sub * nkv <= MAX_STATIC_UNITS:
        units = [(t, c) for t in range(n_sub) for c in range(nkv)]
        qk(*units[0], 0)
        for u, (t, c) in enumerate(units):
            if u + 1 < len(units):
                qk(*units[u + 1], (u + 1) % 2)
            softmax_pv(t, c, u % 2)
            if c == nkv - 1:
                finish(t)
    else:
        assert n_sub == 1
        qk(0, 0, 0)

        def pair(i, carry):
            c = 2 * i
            qk(0, c + 1, 1)
            softmax_pv(0, c, 0)
            qk(0, c + 2, 0)
            softmax_pv(0, c + 1, 1)
            return carry

        lax.fori_loop(0, nkv // 2 - 1, pair, 0)
        qk(0, nkv - 1, 1)
        softmax_pv(0, nkv - 2, 0)
        softmax_pv(0, nkv - 1, 1)
        finish(0)


def attention_t(qs, gs, cos_t, sin_t, ks, vt, n_heads, group, tq, n_sub):
    b, s, _ = qs[0].shape
    skv = ks[0].shape[1]
    nkv, tk = vt.shape[1], vt.shape[3]
    n_kv = n_heads // group
    dv = vt.shape[2] // n_kv
    nparts = len(qs)
    tqb = tq * n_sub
    assert nkv % 2 == 0 and nkv * tk == skv and s % tqb == 0
    in_specs = [pl.BlockSpec((None, tqb, HEAD_DIM), lambda bi, h, qi: (bi, qi, h))]
    if nparts > 1:
        in_specs.append(pl.BlockSpec((None, tqb, HEAD_DIM), lambda bi, h, qi: (bi, qi, h // 2)))
    in_specs += [pl.BlockSpec(g.shape, lambda bi, h, qi: (0, 0)) for g in gs]
    in_specs += [pl.BlockSpec((cos_t.shape[0], tqb), lambda bi, h, qi: (0, qi))] * 2
    for k in ks:
        if k.shape[2] == n_kv * HEAD_DIM:
            in_specs.append(pl.BlockSpec((None, skv, HEAD_DIM), lambda bi, h, qi: (bi, 0, h // group)))
        else:
            in_specs.append(pl.BlockSpec((None, skv, HEAD_DIM), lambda bi, h, qi: (bi, 0, 0)))
    in_specs.append(pl.BlockSpec((None, nkv, dv, tk), lambda bi, h, qi: (bi, 0, h // group, 0)))
    scratch = []
    if nparts > 1:
        scratch += [pltpu.VMEM((skv, nparts * HEAD_DIM), BF16), pltpu.VMEM((HEAD_DIM, tq), F32)]
    scratch += [pltpu.VMEM((nkv, dv + ONES_ROWS, tk), BF16), pltpu.VMEM((2, tk, tq), F32),
                pltpu.VMEM((n_sub, 1, tq), F32), pltpu.VMEM((n_sub, dv + ONES_ROWS, tq), F32)]
    kern = functools.partial(_attn_t_kernel, nparts=nparts, tk=tk, nkv=nkv, n_sub=n_sub)
    return pl.pallas_call(
        kern,
        grid=(b, n_heads, s // tqb),
        in_specs=in_specs,
        out_specs=pl.BlockSpec((None, tqb, dv), lambda bi, h, qi: (bi, qi, h)),
        out_shape=jax.ShapeDtypeStruct((b, s, n_heads * dv), BF16),
        scratch_shapes=scratch,
        compiler_params=_params(),
        name="attention_t",
    )(*qs, *gs, cos_t, sin_t, *ks, vt)


def _mm_vt_kernel(*refs, scaled):
    a_ref, wt_ref, o_ref = refs[0], refs[1], refs[-1]
    out = lax.dot_general(wt_ref[...], a_ref[...], (((1,), (1,)), ((), ())), preferred_element_type=F32)
    if scaled:
        out = out * refs[2][...].T[:1, :]
    o_ref[...] = out.astype(o_ref.dtype)


def mm_vt(a, wt, tk, tn, r=None):
    m, k = a.shape
    cols = wt.shape[0]
    tr = min(tk, 512 if k > 1024 else 2048)
    per = tk // tr
    in_specs = [pl.BlockSpec((tr, k), lambda i, j: (i, 0)),
                pl.BlockSpec((tn, k), lambda i, j: (j, 0))]
    args = [a, wt]
    if r is not None:
        in_specs.append(_r_spec(tr))
        args.append(r)
    return pl.pallas_call(
        functools.partial(_mm_vt_kernel, scaled=r is not None),
        grid=(m // tr, cols // tn),
        in_specs=in_specs,
        out_specs=pl.BlockSpec((None, tn, tr), lambda i, j: (i // per, j, i % per)),
        out_shape=jax.ShapeDtypeStruct((m // tk, cols, tk), BF16),
        compiler_params=_params(),
        name="mm_vt",
    )(*args)


def _deinterleave(w, width):
    lead = w.shape[:-1]
    n = w.shape[-1] // width
    w = w.reshape(lead + (n, width // 2, 2))
    w = jnp.swapaxes(w, -1, -2)
    return w.reshape(lead + (n * width,))


def _pad_groups(w, width, to):
    lead = w.shape[:-1]
    n = w.shape[-1] // width
    w = w.reshape(lead + (n, width))
    w = jnp.pad(w, [(0, 0)] * len(lead) + [(0, 0), (0, to - width)])
    return w.reshape(lead + (n * to,))


def _rope_angles(seq_len, rot_dim):
    rows = seq_len // GRID_W
    row, col = jnp.meshgrid(jnp.arange(rows), jnp.arange(GRID_W), indexing="ij")
    row = row.reshape(-1).astype(F32)
    col = col.reshape(-1).astype(F32)
    axis_dim = rot_dim // 2
    inv_freq = ROPE_THETA ** (-jnp.arange(0, axis_dim, 2, dtype=F32) / axis_dim)
    ang = jnp.concatenate([row[:, None] * inv_freq, col[:, None] * inv_freq], axis=-1)
    return jnp.cos(ang), jnp.sin(ang)


def _rope_tables(seq_len, rot_dim, width):
    cos, sin = _rope_angles(seq_len, rot_dim)
    half = rot_dim // 2
    z = jnp.zeros((seq_len, width - rot_dim), F32)
    zh = jnp.zeros_like(sin)
    c = jnp.concatenate([cos, cos, z], axis=-1)
    if rot_dim == width:
        return (half,), [c, jnp.concatenate([-sin, sin], axis=-1)]
    s_a = jnp.concatenate([-sin, zh, z], axis=-1)
    s_b = jnp.concatenate([zh, sin, z], axis=-1)
    return (width - half, half), [c, s_a, s_b]


def kernel(x, mem, ffn1_norm, ffn1_w_gu, ffn1_w_down, mix_norm, w_o, mem_norm, w_mem_kv, mem_q_norm, mem_k_norm, ffn2_norm, ffn2_w_gu, ffn2_w_down, a_w_in, a_q_a_norm, a_kv_a_norm, a_w_q_b, a_w_kv_b, a_q_nope_norm, a_q_pe_norm, a_k_nope_norm, a_k_pe_norm, b_w_in, b_q_norm, b_k_norm):
    b, s, d = x.shape
    n = b * s
    m_tok = mem.shape[1]
    depth = ffn1_norm.shape[0]
    hd = HEAD_DIM
    mem_w = w_mem_kv.shape[2] // 2
    mem_hd = mem_w // MEM_HEADS
    tok_w = w_o.shape[1] - mem_w
    n_heads = tok_w // hd
    q_lora = a_q_a_norm.shape[1]
    kv_lora = a_kv_a_norm.shape[1]
    n_kv = n_heads // B_GROUP

    tm = min(1024, s)
    seq_tiles = s // tm
    tq = min(1024, s)
    tk = min(2048, s // 4)
    n_sub = 2 if s // tq >= 2 else 1
    tm_mem = min(512, b * m_tok)
    tm_norm = min(256, s)

    def tile(width, pref):
        t = min(pref, width)
        while width % t:
            t -= hd
        return t

    rope_a = _rope_tables(s, A_ROPE, hd)
    rope_b = _rope_tables(s, hd, hd)
    cos_at, sin_at = (t.T for t in _rope_angles(s, A_ROPE))
    cos_bt, sin_bt = (t.T for t in _rope_angles(s, hd))
    lane_bcast = lambda g: jnp.broadcast_to(g[:, None], (g.shape[0], LANES))
    scale_a = float((hd + A_ROPE) ** -0.5) * LOG2E
    scale_b = float(hd ** -0.5) * LOG2E
    scale_m = float(mem_hd ** -0.5)

    xf = x.reshape(n, d)
    memf = mem.reshape(b * m_tok, d)
    w_gu1, w_gu2 = ffn1_w_gu.astype(BF16), ffn2_w_gu.astype(BF16)
    w_dn1, w_dn2 = ffn1_w_down.astype(BF16), ffn2_w_down.astype(BF16)
    w_ob = w_o.astype(BF16)
    d_ff = w_gu1.shape[2] // 2
    r3 = lambda t: t.reshape(b, s, t.shape[-1])

    def produce(a_list, w, scale, g_next):
        x_new, xg, ss = mm_residual(a_list, w, i, xf, scale, g_next, tm, tile(d, 512))
        return x_new, xg, row_scale(ss, d, tm)

    xg, r = scale_prep(xf, ffn1_norm[0], tm_norm)
    for i in range(depth):
        act = mm_swiglu(xg, r, w_gu1, i, tm, tile(d_ff, 768))
        xf, xg, r = produce([act], w_dn1, 0.5, mix_norm[i])
        proj = functools.partial(mm_headnorm, r=r)
        j = i // N_MIXERS
        if i % N_MIXERS == 0:
            w_in = a_w_in[j]
            o1, o2, o3 = q_lora, q_lora + kv_lora, q_lora + kv_lora + A_ROPE
            w_cq = w_in[:, :o1].astype(BF16)
            w_ckv = w_in[:, o1:o2].astype(BF16)
            w_kpe = _pad_groups(_deinterleave(w_in[:, o2:o3], A_ROPE), A_ROPE, hd).astype(BF16)
            w_qm = w_in[:, o3:].astype(BF16)
            wq = a_w_q_b[j].reshape(q_lora, n_heads, hd + A_ROPE)
            w_qn = wq[:, :, :hd].reshape(q_lora, n_heads * hd).astype(BF16)
            w_qp = _deinterleave(wq[:, :, hd:].reshape(q_lora, n_heads * A_ROPE), A_ROPE).astype(BF16)
            wkv = a_w_kv_b[j].reshape(kv_lora, n_heads, 2 * hd)
            w_kn = wkv[:, :, :hd].reshape(kv_lora, n_heads * hd).astype(BF16)
            w_v = wkv[:, :, hd:].reshape(kv_lora, n_heads * hd).astype(BF16)
            g_qpe = _deinterleave(a_q_pe_norm[j], A_ROPE) * scale_a
            g_kpe = _pad_groups(_deinterleave(a_k_pe_norm[j], A_ROPE), A_ROPE, hd)

            cqn = proj(xg, w_cq, a_q_a_norm[j], q_lora, q_lora, tm, q_lora)
            ckvn = proj(xg, w_ckv, a_kv_a_norm[j], kv_lora, kv_lora, tm, kv_lora)
            kpe = proj(xg, w_kpe, g_kpe, hd, A_ROPE, tm, hd, rope=rope_a, seq_tiles=seq_tiles)
            mq = proj(xg, w_qm, mem_q_norm[i] * scale_m, mem_hd, mem_hd, tm, tile(mem_w, 1024))
            tw = tile(n_heads * hd, 1024)
            qn = mm_cast(cqn, w_qn, tm, tw, F32)
            qp = mm_cast(cqn, w_qp, tm, tile(n_heads * A_ROPE, 1024), F32)
            kn = mm_headnorm(ckvn, w_kn, a_k_nope_norm[j], hd, hd, tm, tw)
            vt = mm_vt(ckvn, w_v.T, tk, tw).reshape(b, s // tk, n_heads * hd, tk)
            tok = attention_t([r3(qn), r3(qp)], [lane_bcast(a_q_nope_norm[j] * scale_a), lane_bcast(g_qpe)],
                              cos_at, sin_at, [r3(kn), r3(kpe)], vt, n_heads, 1, tq, n_sub)
        else:
            w_in = b_w_in[j]
            qw, kw = n_heads * hd, n_kv * hd
            w_q = _deinterleave(w_in[:, :qw], hd).astype(BF16)
            w_k = _deinterleave(w_in[:, qw:qw + kw], hd).astype(BF16)
            w_v = w_in[:, qw + kw:qw + 2 * kw].astype(BF16)
            w_qm = w_in[:, qw + 2 * kw:].astype(BF16)
            g_q = _deinterleave(b_q_norm[j], hd) * scale_b
            g_k = _deinterleave(b_k_norm[j], hd)
            q = mm_cast(xg, w_q, tm, tile(qw, 1024), F32, r=r)
            k = proj(xg, w_k, g_k, hd, hd, tm, tile(kw, 1024), rope=rope_b, seq_tiles=seq_tiles)
            vt = mm_vt(xg, w_v.T, tk, tile(kw, 1024), r=r).reshape(b, s // tk, kw, tk)
            mq = proj(xg, w_qm, mem_q_norm[i] * scale_m, mem_hd, mem_hd, tm, tile(mem_w, 1024))
            tok = attention_t([r3(q)], [lane_bcast(g_q)], cos_bt, sin_bt, [r3(k)], vt, n_heads, B_GROUP, tq, n_sub)

        memn = rmsnorm(memf, mem_norm[i], tm_mem)
        w_mkv = w_mem_kv[i].astype(BF16)
        mk = mm_headnorm(memn, w_mkv[:, :mem_w], mem_k_norm[i], mem_hd, mem_hd, tm_mem, tile(mem_w, 1024))
        mv = mm_cast(memn, w_mkv[:, mem_w:], tm_mem, tile(mem_w, 1024))
        mo = attention([mq.reshape(b, s, mem_w)], [mk.reshape(b, m_tok, mem_w)], mv.reshape(b, m_tok, mem_w),
                       MEM_HEADS, 1, mem_hd, mem_hd, tq, m_tok)
        xf, xg, r = produce([tok.reshape(n, tok_w), mo.reshape(n, mem_w)], w_ob, 1.0, ffn2_norm[i])
        act = mm_swiglu(xg, r, w_gu2, i, tm, tile(d_ff, 768))
        if i + 1 < depth:
            xf, xg, r = produce([act], w_dn2, 0.5, ffn1_norm[i + 1])
        else:
            xf = mm_residual([act], w_dn2, i, xf, 0.5, None, tm, tile(d, 512))
    return xf.reshape(b, s, d)
```
